```python
import jax
import jax.numpy as jnp
from jax import lax
import numpy as np

D_MODEL = 2048
BATCH = 1
SEQ = 8192
DEPTH = 4

HEAD_DIM = 128
N_HEADS_NSA = 8
N_KV_NSA = 2
N_HEADS_SB = 8
D_MIX = (N_HEADS_NSA + N_HEADS_SB) * HEAD_DIM
ROPE_DIM = HEAD_DIM // 4
ROPE_THETA = 500000.0
CMP_BLOCK = 32
CMP_STRIDE = 16
CMP_HIDDEN = 256
SLC_BLOCK = 64
SLC_TOP_N = 16
WINDOW = 512
Q_BLOCK = 128
N_BRANCH = 3
D_FF = ((8 * D_MODEL + 3 * 256 - 1) // (3 * 256)) * 256
EPS = 1e-6
NEG_INF = -1e30
FORCE_SCORE = 1e4
NEG_SCORE = -1e4
SPLIT_SIZES = [N_HEADS_NSA * HEAD_DIM] + [N_KV_NSA * HEAD_DIM] * 6 + [N_BRANCH * N_HEADS_NSA] + [N_HEADS_SB * HEAD_DIM] * 3
D_IN = sum(SPLIT_SIZES)

kernel_name = 'hybrid_nsa_stickbreaking_trunk'


def rms_norm(x, g):
    xf = x.astype(jnp.float32)
    y = xf * lax.rsqrt(jnp.mean(xf * xf, axis=-1, keepdims=True) + EPS)
    return (y * g.astype(jnp.float32)).astype(x.dtype)


def rope_tables(pos):
    inv = ROPE_THETA ** (-jnp.arange(0, ROPE_DIM, 2, dtype=jnp.float32) / ROPE_DIM)
    ang = pos.astype(jnp.float32)[..., None] * inv
    return jnp.cos(ang)[:, None], jnp.sin(ang)[:, None]


def apply_partial_rope(x, cos, sin):
    half = ROPE_DIM // 2
    x1 = x[..., :half].astype(jnp.float32)
    x2 = x[..., half:ROPE_DIM].astype(jnp.float32)
    r1 = (x1 * cos - x2 * sin).astype(x.dtype)
    r2 = (x2 * cos + x1 * sin).astype(x.dtype)
    return jnp.concatenate([r1, r2, x[..., ROPE_DIM:]], axis=-1)


def masked_softmax(s, mask):
    s = jnp.where(mask, s.astype(jnp.float32), NEG_INF)
    p = jax.nn.softmax(s, axis=-1)
    return jnp.where(mask, p, 0.0)


def unblock(o, axis):
    o = jnp.moveaxis(o, 0, axis)
    shp = o.shape
    return o.reshape(shp[:axis] + (shp[axis] * shp[axis + 1],) + shp[axis + 2:])


def nsa_attention(q, k_c, v_c, k_s, v_s, k_w, v_w, gates, positions, cos, sin,
                  cmp_pos, w_ck1, w_ck2, w_cv1, w_cv2):
    B, H, S, d = q.shape
    G = k_c.shape[1]
    R = H // G
    scale = d ** -0.5
    q = apply_partial_rope(q, cos, sin)
    k_s = apply_partial_rope(k_s, cos, sin)
    k_w = apply_partial_rope(k_w, cos, sin)
    qg = q.reshape(B, G, R, S, d)
    t_idx = jnp.arange(S)

    n_cmp = (S - CMP_BLOCK) // CMP_STRIDE + 1
    blk_idx = jnp.arange(n_cmp)[:, None] * CMP_STRIDE + jnp.arange(CMP_BLOCK)[None, :]

    def compress(t, w1, w2):
        blocks = t[:, :, blk_idx, :] + cmp_pos
        flat = blocks.reshape(B, G, n_cmp, CMP_BLOCK * d)
        return jax.nn.gelu(flat @ w1) @ w2

    kc = compress(k_c, w_ck1, w_ck2)
    vc = compress(v_c, w_cv1, w_cv2)
    cmp_end = jnp.arange(n_cmp) * CMP_STRIDE + CMP_BLOCK - 1
    ccos, csin = rope_tables(positions[:, cmp_end])
    kc = apply_partial_rope(kc, ccos, csin)
    s_cmp = jnp.einsum('bgrsd,bgnd->bgrsn', qg, kc) * scale
    mask_cmp = cmp_end[None, :] <= t_idx[:, None]
    p_cmp = masked_softmax(s_cmp, mask_cmp)
    o_cmp = jnp.einsum('bgrsn,bgnd->bgrsd', p_cmp.astype(vc.dtype), vc)

    n_slc = S // SLC_BLOCK
    nb = CMP_BLOCK // CMP_STRIDE
    ratio = SLC_BLOCK // CMP_STRIDE
    p_imp = p_cmp.sum(axis=2)
    p_ext = jnp.pad(p_imp, ((0, 0), (0, 0), (0, 0), (nb - 1, ratio * n_slc - n_cmp)))
    q_imp = p_ext[..., nb - 1: nb - 1 + ratio * n_slc]
    for n in range(1, nb):
        q_imp = q_imp + p_ext[..., nb - 1 - n: nb - 1 - n + ratio * n_slc]
    p_slc = q_imp.reshape(B, G, S, n_slc, ratio).sum(axis=-1)
    blk_t = t_idx // SLC_BLOCK
    j = jnp.arange(n_slc)
    causal_blk = j[None, :] <= blk_t[:, None]
    forced = (j[None, :] == 0) | (j[None, :] == blk_t[:, None]) | (j[None, :] == blk_t[:, None] - 1)
    imp = jnp.where(causal_blk, jnp.where(forced, FORCE_SCORE, p_slc), NEG_SCORE)
    top_n = min(SLC_TOP_N, n_slc)
    sel_val, sel_idx = lax.top_k(imp, top_n)
    sel_ok = sel_val > 0.5 * NEG_SCORE

    k_blocks = k_s.reshape(B, G, n_slc, SLC_BLOCK, d)
    v_blocks = v_s.reshape(B, G, n_slc, SLC_BLOCK, d)
    k_w_pad = jnp.pad(k_w, ((0, 0), (0, 0), (WINDOW, 0), (0, 0)))
    v_w_pad = jnp.pad(v_w, ((0, 0), (0, 0), (WINDOW, 0), (0, 0)))
    bi = jnp.arange(B)[:, None, None, None]
    gi = jnp.arange(G)[None, :, None, None]
    off = jnp.arange(SLC_BLOCK)
    n_keys = top_n * SLC_BLOCK

    def block(nq):
        q0 = nq * Q_BLOCK
        qb = lax.dynamic_slice_in_dim(qg, q0, Q_BLOCK, axis=3)
        tq = q0 + jnp.arange(Q_BLOCK)
        idx = lax.dynamic_slice_in_dim(sel_idx, q0, Q_BLOCK, axis=2)
        ok = lax.dynamic_slice_in_dim(sel_ok, q0, Q_BLOCK, axis=2)
        ks = k_blocks[bi, gi, idx].reshape(B, G, Q_BLOCK, n_keys, d)
        vs = v_blocks[bi, gi, idx].reshape(B, G, Q_BLOCK, n_keys, d)
        kpos = (idx[..., None] * SLC_BLOCK + off).reshape(B, G, Q_BLOCK, n_keys)
        kok = jnp.broadcast_to(ok[..., None], (B, G, Q_BLOCK, top_n, SLC_BLOCK)).reshape(B, G, Q_BLOCK, n_keys)
        m_s = kok & (kpos <= tq[:, None])
        s_s = jnp.einsum('bgrqd,bgqkd->bgrqk', qb, ks) * scale
        p_s = masked_softmax(s_s, m_s[:, :, None])
        o_s = jnp.einsum('bgrqk,bgqkd->bgrqd', p_s.astype(vs.dtype), vs)
        kw = lax.dynamic_slice_in_dim(k_w_pad, q0, WINDOW + Q_BLOCK, axis=2)
        vw = lax.dynamic_slice_in_dim(v_w_pad, q0, WINDOW + Q_BLOCK, axis=2)
        wpos = q0 - WINDOW + jnp.arange(WINDOW + Q_BLOCK)
        diff = tq[:, None] - wpos[None, :]
        m_w = (wpos[None, :] >= 0) & (diff >= 0) & (diff < WINDOW)
        s_w = jnp.einsum('bgrqd,bgkd->bgrqk', qb, kw) * scale
        p_w = masked_softmax(s_w, m_w)
        o_w = jnp.einsum('bgrqk,bgkd->bgrqd', p_w.astype(vw.dtype), vw)
        return o_s, o_w

    o_slc, o_win = lax.map(block, jnp.arange(S // Q_BLOCK))
    o_slc = unblock(o_slc, 3).reshape(B, H, S, d)
    o_win = unblock(o_win, 3).reshape(B, H, S, d)
    o_cmp = o_cmp.reshape(B, H, S, d)
    return gates[:, 0] * o_cmp + gates[:, 1] * o_slc + gates[:, 2] * o_win


def stick_breaking_attention(q, k, v):
    B, H, S, d = q.shape
    scale = d ** -0.5
    kpos = jnp.arange(S)

    def block(nq):
        q0 = nq * Q_BLOCK
        qb = lax.dynamic_slice_in_dim(q, q0, Q_BLOCK, axis=2)
        tq = q0 + jnp.arange(Q_BLOCK)
        valid = kpos[None, :] < tq[:, None]
        z = jnp.einsum('bhqd,bhkd->bhqk', qb, k).astype(jnp.float32) * scale
        log_1m = jnp.where(valid, jax.nn.log_sigmoid(-z), 0.0)
        after = lax.cumsum(log_1m, axis=3, reverse=True) - log_1m
        a = jnp.where(valid, jnp.exp(jax.nn.log_sigmoid(z) + after), 0.0)
        return jnp.einsum('bhqk,bhkd->bhqd', a.astype(v.dtype), v)

    o = lax.map(block, jnp.arange(S // Q_BLOCK))
    return unblock(o, 2)


def hybrid_mixer(xn, positions, cos, sin, w_in, b_gate, cmp_pos, w_ck1, w_ck2,
                 w_cv1, w_cv2, g_grp, w_out):
    B, S, _ = xn.shape
    proj = xn @ w_in
    offsets = [int(o) for o in np.cumsum(SPLIT_SIZES)[:-1]]
    q_n, kc, vc, ksl, vsl, kw, vw, g_logit, q_sb, k_sb, v_sb = jnp.split(proj, offsets, axis=-1)

    def heads(t, h):
        return t.reshape(B, S, h, HEAD_DIM).transpose(0, 2, 1, 3)

    gates = jax.nn.sigmoid((g_logit + b_gate).astype(jnp.float32))
    gates = gates.reshape(B, S, N_BRANCH, N_HEADS_NSA).transpose(0, 2, 3, 1)[..., None]
    o_nsa = nsa_attention(heads(q_n, N_HEADS_NSA), heads(kc, N_KV_NSA), heads(vc, N_KV_NSA),
                          heads(ksl, N_KV_NSA), heads(vsl, N_KV_NSA), heads(kw, N_KV_NSA),
                          heads(vw, N_KV_NSA), gates, positions, cos, sin, cmp_pos,
                          w_ck1, w_ck2, w_cv1, w_cv2)
    o_sb = stick_breaking_attention(heads(q_sb, N_HEADS_SB), heads(k_sb, N_HEADS_SB), heads(v_sb, N_HEADS_SB))
    o = jnp.concatenate([o_nsa.astype(jnp.float32), o_sb.astype(jnp.float32)], axis=1)
    o = o.transpose(0, 2, 1, 3)
    o = o * lax.rsqrt(jnp.mean(o * o, axis=-1, keepdims=True) + EPS)
    o = o.reshape(B, S, D_MIX) * g_grp.astype(jnp.float32)
    return o.astype(xn.dtype) @ w_out


def swiglu(xn, w_gate, w_up, w_down):
    return (jax.nn.silu(xn @ w_gate) * (xn @ w_up)) @ w_down


def setup_inputs(seed: int = 0) -> dict:
    key = jax.random.key(seed)
    ks = jax.random.split(key, 16)
    f32 = jnp.float32

    def nrm(k, shape, scale):
        return jax.random.normal(k, shape, f32) * scale

    L = DEPTH
    return {
        'x': nrm(ks[0], (BATCH, SEQ, D_MODEL), 1.0),
        'positions': jnp.broadcast_to(jnp.arange(SEQ, dtype=jnp.int32)[None, :], (BATCH, SEQ)),
        'norm_mix': 1.0 + nrm(ks[1], (L, D_MODEL), 0.02),
        'w_in': nrm(ks[2], (L, D_MODEL, D_IN), D_MODEL ** -0.5),
        'b_gate': nrm(ks[3], (L, N_BRANCH * N_HEADS_NSA), 0.01),
        'cmp_pos': nrm(ks[4], (L, CMP_BLOCK, HEAD_DIM), 0.1),
        'w_cmp_k1': nrm(ks[5], (L, CMP_BLOCK * HEAD_DIM, CMP_HIDDEN), (CMP_BLOCK * HEAD_DIM) ** -0.5),
        'w_cmp_k2': nrm(ks[6], (L, CMP_HIDDEN, HEAD_DIM), CMP_HIDDEN ** -0.5),
        'w_cmp_v1': nrm(ks[7], (L, CMP_BLOCK * HEAD_DIM, CMP_HIDDEN), (CMP_BLOCK * HEAD_DIM) ** -0.5),
        'w_cmp_v2': nrm(ks[8], (L, CMP_HIDDEN, HEAD_DIM), CMP_HIDDEN ** -0.5),
        'norm_grp': 1.0 + nrm(ks[9], (L, D_MIX), 0.02),
        'w_out': nrm(ks[10], (L, D_MIX, D_MODEL), D_MIX ** -0.5),
        'norm_ffn': 1.0 + nrm(ks[11], (L, D_MODEL), 0.02),
        'w_gate': nrm(ks[12], (L, D_MODEL, D_FF), D_MODEL ** -0.5),
        'w_up': nrm(ks[13], (L, D_MODEL, D_FF), D_MODEL ** -0.5),
        'w_down': nrm(ks[14], (L, D_FF, D_MODEL), D_FF ** -0.5),
        'norm_final': 1.0 + nrm(ks[15], (D_MODEL,), 0.02),
    }


def reference(x, positions, norm_mix, w_in, b_gate, cmp_pos, w_cmp_k1, w_cmp_k2,
              w_cmp_v1, w_cmp_v2, norm_grp, w_out, norm_ffn, w_gate, w_up, w_down,
              norm_final):
    cos, sin = rope_tables(positions)
    h = x
    for layer in range(DEPTH):
        xn = rms_norm(h, norm_mix[layer])
        h = h + hybrid_mixer(xn, positions, cos, sin, w_in[layer], b_gate[layer], cmp_pos[layer],
                             w_cmp_k1[layer], w_cmp_k2[layer], w_cmp_v1[layer], w_cmp_v2[layer],
                             norm_grp[layer], w_out[layer]).astype(h.dtype)
        xn = rms_norm(h, norm_ffn[layer])
        h = h + swiglu(xn, w_gate[layer], w_up[layer], w_down[layer]).astype(h.dtype)
    return rms_norm(h, norm_final)
```

```python
import functools

import jax
import jax.numpy as jnp
import numpy as np
from jax import lax
from jax.experimental import pallas as pl
from jax.experimental.pallas import tpu as pltpu

HEAD_DIM = 128
N_HEADS_NSA = 8
N_KV_NSA = 2
GROUP = N_HEADS_NSA // N_KV_NSA
N_HEADS_SB = 8
ROPE_DIM = HEAD_DIM // 4
ROPE_HALF = ROPE_DIM // 2
ROPE_THETA = 500000.0
CMP_BLOCK = 32
CMP_STRIDE = 16
CMP_HIDDEN = 256
SLC_BLOCK = 64
SLC_TOP_N = 16
WINDOW = 512
N_BRANCH = 3
EPS = 1e-6
NEG_INF = -1e30
FORCE_SCORE = 1e4
NEG_SCORE = -1e4
SCALE = HEAD_DIM ** -0.5

LANES = 128
SEL_MASK_BIAS = -30000.0
SB_UNDERFLOW = -105.0
VMEM_LIMIT = 52 * 1024 * 1024

Q_BLK = 128
KV_BLK = 512
SB_KV_BLK = 128

COL_Q = 0
COL_KC = 8
COL_KS, COL_VS, COL_KW, COL_VW = 12, 14, 16, 18
COL_SBQ, COL_SBK, COL_SBV = 20, 28, 36
N_MAIN = 44 * HEAD_DIM

f32 = jnp.float32
bf16 = jnp.bfloat16


def _cparams(*sem):
    return pltpu.CompilerParams(dimension_semantics=sem, vmem_limit_bytes=VMEM_LIMIT)


def _dot(a, b):
    return jnp.dot(a, b, preferred_element_type=f32)


def _dot_nt(a, b):
    return lax.dot_general(a, b, (((1,), (1,)), ((), ())), preferred_element_type=f32)


def _split_dot(a, b):
    hi = a.astype(bf16)
    lo = (a - hi.astype(f32)).astype(bf16)
    return _dot(hi, b) + _dot(lo, b)


def _rope(x, cosf, sina, sinb):
    w = x.shape[-1]
    return (x * cosf + pltpu.roll(x, w - ROPE_HALF, axis=1) * sina
            + pltpu.roll(x, ROPE_HALF, axis=1) * sinb)


PROJ_TN = 256


def _proj_kernel(h_ref, g_ref, w_ref, wg_ref, bg_ref, cosf_ref, sina_ref, sinb_ref,
                 o_ref, gate_ref, xn_ref):
    j = pl.program_id(1)

    @pl.when(j == 0)
    def _():
        x = h_ref[...]
        y = x * lax.rsqrt(jnp.mean(x * x, axis=-1, keepdims=True) + EPS)
        xn_ref[...] = (y * g_ref[...]).astype(bf16)
        gate_ref[...] = jax.nn.sigmoid(_dot(xn_ref[...], wg_ref[...]) + bg_ref[...])

    acc = _dot(xn_ref[...], w_ref[...])
    q_tiles = N_HEADS_NSA * HEAD_DIM // PROJ_TN
    is_q = j < q_tiles
    is_k_rope = (j == COL_KS * HEAD_DIM // PROJ_TN) | (j == COL_KW * HEAD_DIM // PROJ_TN)
    is_sbq = (j >= COL_SBQ * HEAD_DIM // PROJ_TN) & (j < COL_SBK * HEAD_DIM // PROJ_TN)

    @pl.when(is_q)
    def _():
        r = _rope(acc, cosf_ref[...], sina_ref[...], sinb_ref[...])
        o_ref[...] = (r * SCALE).astype(bf16)

    @pl.when(is_k_rope)
    def _():
        o_ref[...] = _rope(acc, cosf_ref[...], sina_ref[...], sinb_ref[...]).astype(bf16)

    @pl.when(is_sbq)
    def _():
        o_ref[...] = (acc * SCALE).astype(bf16)

    @pl.when(jnp.logical_not(is_q | is_k_rope | is_sbq))
    def _():
        o_ref[...] = acc.astype(bf16)


def _proj(h, g, w, wg, bg, cosf, sina, sinb, tm):
    S, D = h.shape
    N = w.shape[1]
    tn = PROJ_TN
    return pl.pallas_call(
        _proj_kernel,
        grid=(S // tm, N // tn),
        in_specs=[
            pl.BlockSpec((tm, D), lambda i, j: (i, 0)),
            pl.BlockSpec((1, D), lambda i, j: (0, 0)),
            pl.BlockSpec((D, tn), lambda i, j: (0, j)),
            pl.BlockSpec((D, LANES), lambda i, j: (0, 0)),
            pl.BlockSpec((1, LANES), lambda i, j: (0, 0)),
            pl.BlockSpec((tm, tn), lambda i, j: (i, 0)),
            pl.BlockSpec((tm, tn), lambda i, j: (i, 0)),
            pl.BlockSpec((tm, tn), lambda i, j: (i, 0)),
        ],
        out_specs=[
            pl.BlockSpec((tm, tn), lambda i, j: (i, j)),
            pl.BlockSpec((tm, LANES), lambda i, j: (i, 0)),
        ],
        out_shape=[
            jax.ShapeDtypeStruct((S, N), bf16),
            jax.ShapeDtypeStruct((S, LANES), f32),
        ],
        scratch_shapes=[pltpu.VMEM((tm, D), bf16)],
        compiler_params=_cparams("parallel", "arbitrary"),
        name="proj",
    )(h, g, w, wg, bg, cosf, sina, sinb)


def _gelu_tanh(x):
    return 0.5 * x * (1.0 + jnp.tanh(np.sqrt(2.0 / np.pi) * (x + 0.044715 * (x * x * x))))


def _compress_kernel(t_ref, w1_ref, w2_ref, cp_ref, cosf_ref, sina_ref, sinb_ref, o_ref):
    i = pl.program_id(0)
    t = t_ref[0]
    half = CMP_STRIDE * HEAD_DIM
    a = _dot(t, w1_ref[0, :half, :])
    b = _dot(t, w1_ref[0, half:, :])
    n_pad = a.shape[0]
    bias = _dot(cp_ref[...], w1_ref[0])[0:1, :]
    hid = a + pltpu.roll(b, n_pad - 1, axis=0) + bias
    c = _dot(_gelu_tanh(hid).astype(bf16), w2_ref[0])

    @pl.when(i < N_KV_NSA)
    def _():
        o_ref[0] = _rope(c, cosf_ref[...], sina_ref[...], sinb_ref[...]).astype(bf16)

    @pl.when(i >= N_KV_NSA)
    def _():
        o_ref[0] = c.astype(bf16)


def _compress(t2, w1, w2, cp, ccosf, csina, csinb):
    n4, n_pad, width = t2.shape
    return pl.pallas_call(
        _compress_kernel,
        grid=(n4,),
        in_specs=[
            pl.BlockSpec((1, n_pad, width), lambda i: (i, 0, 0)),
            pl.BlockSpec((1, 2 * width, CMP_HIDDEN), lambda i: (i // N_KV_NSA, 0, 0)),
            pl.BlockSpec((1, CMP_HIDDEN, HEAD_DIM), lambda i: (i // N_KV_NSA, 0, 0)),
            pl.BlockSpec((8, 2 * width), lambda i: (0, 0)),
            pl.BlockSpec((n_pad, HEAD_DIM), lambda i: (0, 0)),
            pl.BlockSpec((n_pad, HEAD_DIM), lambda i: (0, 0)),
            pl.BlockSpec((n_pad, HEAD_DIM), lambda i: (0, 0)),
        ],
        out_specs=pl.BlockSpec((1, n_pad, HEAD_DIM), lambda i: (i, 0, 0)),
        out_shape=jax.ShapeDtypeStruct((n4, n_pad, HEAD_DIM), bf16),
        compiler_params=_cparams("parallel"),
        name="compress",
    )(t2, w1, w2, cp, ccosf, csina, csinb)


def _cmp_topk_kernel(q_ref, kc_ref, vc_ref, m_ref, o_ref, selb_ref, impt_ref, *, n_slc):
    i = pl.program_id(1)
    t0 = i * Q_BLK
    rows = GROUP * Q_BLK
    kc = kc_ref[0]
    vc = vc_ref[0]
    n_pad = kc.shape[0]
    q4 = jnp.concatenate([q_ref[:, r * HEAD_DIM:(r + 1) * HEAD_DIM] for r in range(GROUP)], axis=0)
    s = _dot_nt(q4, kc)
    t = t0 + (lax.broadcasted_iota(jnp.int32, (rows, n_pad), 0) & (Q_BLK - 1))
    cmp_end = lax.broadcasted_iota(jnp.int32, (rows, n_pad), 1) * CMP_STRIDE + (CMP_BLOCK - 1)
    mask = cmp_end <= t
    s = jnp.where(mask, s, NEG_INF)
    m = jnp.max(s, axis=-1, keepdims=True)
    p = jnp.where(mask, jnp.exp(s - m), 0.0)
    l = jnp.sum(p, axis=-1, keepdims=True)
    p = p / jnp.where(l > 0.0, l, 1.0)
    o = _dot(p.astype(bf16), vc)
    for r in range(GROUP):
        o_ref[:, r * HEAD_DIM:(r + 1) * HEAD_DIM] = o[r * Q_BLK:(r + 1) * Q_BLK, :]

    p_imp = p[0:Q_BLK]
    for r in range(1, GROUP):
        p_imp = p_imp + p[r * Q_BLK:(r + 1) * Q_BLK]
    p_slc_t = _split_dot(p_imp, m_ref[...]).T

    blk = lax.broadcasted_iota(jnp.int32, (LANES, Q_BLK), 0)
    blk_t = (t0 + lax.broadcasted_iota(jnp.int32, (LANES, Q_BLK), 1)) >> 6
    causal = blk <= blk_t
    forced = (blk == 0) | (blk == blk_t) | (blk == blk_t - 1)
    imp = jnp.where(causal, jnp.where(forced, FORCE_SCORE, p_slc_t), NEG_SCORE)
    impt_ref[...] = imp

    def body(b, rank):
        row = impt_ref[pl.ds(b, 1), :]
        ahead = (row > imp) | ((row == imp) & (b < blk))
        return rank + jnp.where(ahead, 1.0, 0.0)

    n_live = jnp.minimum((t0 + Q_BLK - 1) // SLC_BLOCK + 1, n_slc)
    rank = lax.fori_loop(0, n_live, body, jnp.zeros((LANES, Q_BLK), f32))
    sel = (rank < float(min(SLC_TOP_N, n_slc))) & causal
    selb_ref[0] = jnp.where(sel, 0.0, SEL_MASK_BIAS).T.astype(bf16)


def _cmp_topk(P, kvc, m_mat, n_slc):
    S = P.shape[0]
    n_pad = kvc.shape[1]
    gw = GROUP * HEAD_DIM
    return pl.pallas_call(
        functools.partial(_cmp_topk_kernel, n_slc=n_slc),
        grid=(N_KV_NSA, S // Q_BLK),
        in_specs=[
            pl.BlockSpec((Q_BLK, gw), lambda g, i: (i, g)),
            pl.BlockSpec((1, n_pad, HEAD_DIM), lambda g, i: (g, 0, 0)),
            pl.BlockSpec((1, n_pad, HEAD_DIM), lambda g, i: (N_KV_NSA + g, 0, 0)),
            pl.BlockSpec((n_pad, LANES), lambda g, i: (0, 0)),
        ],
        out_specs=[
            pl.BlockSpec((Q_BLK, gw), lambda g, i: (i, g)),
            pl.BlockSpec((1, Q_BLK, LANES), lambda g, i: (g, i, 0)),
        ],
        out_shape=[
            jax.ShapeDtypeStruct((S, N_HEADS_NSA * HEAD_DIM), f32),
            jax.ShapeDtypeStruct((N_KV_NSA, S, LANES), bf16),
        ],
        scratch_shapes=[pltpu.VMEM((LANES, Q_BLK), f32)],
        compiler_params=_cparams("parallel", "parallel"),
        name="cmp_topk",
    )(P, kvc, kvc, m_mat)


def _nsa_kernel(q_ref, ks_ref, vs_ref, kw_ref, vw_ref, e_ref, selb_ref, ocmp_ref, gate_ref,
                gg_ref, o_ref, *, kv_blk, win_span):
    i = pl.program_id(1)
    t0 = i * Q_BLK
    rows = GROUP * Q_BLK
    S = ks_ref.shape[0]
    q4 = jnp.concatenate([q_ref[:, r * HEAD_DIM:(r + 1) * HEAD_DIM] for r in range(GROUP)], axis=0)
    selb = selb_ref[0]
    q_aug = jnp.concatenate([q4, jnp.concatenate([selb] * GROUP, axis=0)], axis=1)

    def slc_body(kb, carry):
        m, l, acc = carry
        k0 = pl.multiple_of(kb * kv_blk, kv_blk)
        k_aug = jnp.concatenate([ks_ref[pl.ds(k0, kv_blk), :], e_ref[pl.ds(k0, kv_blk), :]], axis=1)
        s = _dot_nt(q_aug, k_aug)
        t = t0 + (lax.broadcasted_iota(jnp.int32, (rows, kv_blk), 0) & (Q_BLK - 1))
        kpos = k0 + lax.broadcasted_iota(jnp.int32, (rows, kv_blk), 1)
        s = jnp.where(kpos <= t, s, NEG_INF)
        m_new = jnp.maximum(m, jnp.max(s, axis=-1, keepdims=True))
        alpha = jnp.exp(m - m_new)
        p = jnp.exp(s - m_new)
        l = alpha * l + jnp.sum(p, axis=-1, keepdims=True)
        acc = alpha * acc + _dot(p.astype(bf16), vs_ref[pl.ds(k0, kv_blk), :])
        return m_new, l, acc

    n_kb = (t0 + Q_BLK - 1) // kv_blk + 1
    init = (jnp.full((rows, 1), NEG_INF, f32), jnp.zeros((rows, 1), f32), jnp.zeros((rows, HEAD_DIM), f32))
    _, l_s, acc_s = lax.fori_loop(0, n_kb, slc_body, init)
    o_slc = acc_s / l_s

    w0 = pl.multiple_of(jnp.clip(t0 - WINDOW, 0, S - win_span), Q_BLK)
    s = _dot_nt(q4, kw_ref[pl.ds(w0, win_span), :])
    t = t0 + (lax.broadcasted_iota(jnp.int32, (rows, win_span), 0) & (Q_BLK - 1))
    diff = t - (w0 + lax.broadcasted_iota(jnp.int32, (rows, win_span), 1))
    s = jnp.where((diff >= 0) & (diff < WINDOW), s, NEG_INF)
    p = jnp.exp(s - jnp.max(s, axis=-1, keepdims=True))
    l_w = jnp.sum(p, axis=-1, keepdims=True)
    o_win = _dot(p.astype(bf16), vw_ref[pl.ds(w0, win_span), :]) / l_w

    gates = gate_ref[...]
    g = pl.program_id(0)
    for r in range(GROUP):
        rs = slice(r * Q_BLK, (r + 1) * Q_BLK)
        cs = slice(r * HEAD_DIM, (r + 1) * HEAD_DIM)
        h_lane = lax.broadcasted_iota(jnp.int32, gates.shape, 1) - (g * GROUP + r)

        def gate_col(branch):
            return jnp.sum(jnp.where(h_lane == branch * N_HEADS_NSA, gates, 0.0), axis=-1, keepdims=True)

        o = gate_col(0) * ocmp_ref[:, cs] + gate_col(1) * o_slc[rs] + gate_col(2) * o_win[rs]
        o = o * lax.rsqrt(jnp.mean(o * o, axis=-1, keepdims=True) + EPS)
        o_ref[:, cs] = (o * gg_ref[:, cs]).astype(bf16)


def _nsa(P, e_mat, selb, o_cmp, gates, gg, kv_blk, win_span):
    S = P.shape[0]
    gw = GROUP * HEAD_DIM
    full = lambda c: pl.BlockSpec((S, HEAD_DIM), lambda g, i, c=c: (0, c + g))
    return pl.pallas_call(
        functools.partial(_nsa_kernel, kv_blk=kv_blk, win_span=win_span),
        grid=(N_KV_NSA, S // Q_BLK),
        in_specs=[
            pl.BlockSpec((Q_BLK, gw), lambda g, i: (i, g)),
            full(COL_KS), full(COL_VS), full(COL_KW), full(COL_VW),
            pl.BlockSpec((S, LANES), lambda g, i: (0, 0)),
            pl.BlockSpec((1, Q_BLK, LANES), lambda g, i: (g, i, 0)),
            pl.BlockSpec((Q_BLK, gw), lambda g, i: (i, g)),
            pl.BlockSpec((Q_BLK, LANES), lambda g, i: (i, 0)),
            pl.BlockSpec((1, gw), lambda g, i: (0, g)),
        ],
        out_specs=pl.BlockSpec((Q_BLK, gw), lambda g, i: (i, g)),
        out_shape=jax.ShapeDtypeStruct((S, N_HEADS_NSA * HEAD_DIM), bf16),
        compiler_params=_cparams("parallel", "parallel"),
        name="nsa",
    )(P, P, P, P, P, e_mat, selb, o_cmp, gates, gg)


def _sb_kernel(q_ref, k_ref, v_ref, u_ref, gg_ref, o_ref):
    i = pl.program_id(1)
    t0 = i * Q_BLK
    kb_ = SB_KV_BLK
    q = q_ref[...]
    u = u_ref[...]
    t = t0 + lax.broadcasted_iota(jnp.int32, (Q_BLK, kb_), 0)
    lane = lax.broadcasted_iota(jnp.int32, (Q_BLK, kb_), 1)

    def cond(c):
        step, carry_max, _, _ = c
        return (step <= i) & (carry_max > SB_UNDERFLOW)

    def body(c):
        step, _, carry, acc = c
        k0 = pl.multiple_of((i - step) * kb_, kb_)
        z = _dot_nt(q, k_ref[pl.ds(k0, kb_), :])
        valid = (k0 + lane) < t
        log_1m = -(jnp.maximum(z, 0.0) + jnp.log1p(jnp.exp(-jnp.abs(z))))
        log_1m = jnp.where(valid, log_1m, 0.0)
        after = _split_dot(log_1m, u) + carry
        a = jnp.where(valid, jnp.exp(z + log_1m + after), 0.0)
        acc = acc + _dot(a.astype(bf16), v_ref[pl.ds(k0, kb_), :])
        carry = carry + jnp.sum(log_1m, axis=-1, keepdims=True)
        return step + 1, jnp.max(carry), carry, acc

    init = (jnp.int32(0), jnp.float32(0.0), jnp.zeros((Q_BLK, 1), f32), jnp.zeros((Q_BLK, HEAD_DIM), f32))
    _, _, _, o = lax.while_loop(cond, body, init)
    o = o * lax.rsqrt(jnp.mean(o * o, axis=-1, keepdims=True) + EPS)
    o_ref[...] = (o * gg_ref[...]).astype(bf16)


def _sb(P, u_mat, gg):
    S = P.shape[0]
    return pl.pallas_call(
        _sb_kernel,
        grid=(N_HEADS_SB, S // Q_BLK),
        in_specs=[
            pl.BlockSpec((Q_BLK, HEAD_DIM), lambda h, i: (i, COL_SBQ + h)),
            pl.BlockSpec((S, HEAD_DIM), lambda h, i: (0, COL_SBK + h)),
            pl.BlockSpec((S, HEAD_DIM), lambda h, i: (0, COL_SBV + h)),
            pl.BlockSpec((SB_KV_BLK, SB_KV_BLK), lambda h, i: (0, 0)),
            pl.BlockSpec((1, HEAD_DIM), lambda h, i: (0, N_HEADS_NSA + h)),
        ],
        out_specs=pl.BlockSpec((Q_BLK, HEAD_DIM), lambda h, i: (i, h)),
        out_shape=jax.ShapeDtypeStruct((S, N_HEADS_SB * HEAD_DIM), bf16),
        compiler_params=_cparams("parallel", "parallel"),
        name="sb",
    )(P, P, P, u_mat, gg)


def _out_proj_kernel(h_ref, a_ref, b_ref, wa_ref, wb_ref, o_ref):
    o_ref[...] = h_ref[...] + _dot(a_ref[...], wa_ref[...]) + _dot(b_ref[...], wb_ref[...])


def _out_proj(h, o_nsa, o_sb, w_out, tm, tn):
    S, D = h.shape
    ka = o_nsa.shape[1]
    kb = o_sb.shape[1]
    return pl.pallas_call(
        _out_proj_kernel,
        grid=(S // tm, D // tn),
        in_specs=[
            pl.BlockSpec((tm, tn), lambda i, j: (i, j)),
            pl.BlockSpec((tm, ka), lambda i, j: (i, 0)),
            pl.BlockSpec((tm, kb), lambda i, j: (i, 0)),
            pl.BlockSpec((ka, tn), lambda i, j: (0, j)),
            pl.BlockSpec((kb, tn), lambda i, j: (ka // kb, j)),
        ],
        out_specs=pl.BlockSpec((tm, tn), lambda i, j: (i, j)),
        out_shape=jax.ShapeDtypeStruct((S, D), f32),
        compiler_params=_cparams("parallel", "parallel"),
        name="out_proj",
    )(h, o_nsa, o_sb, w_out, w_out)


def _ffn_kernel(h_ref, g_ref, wg_ref, wu_ref, wd_ref, gf_ref, o_ref, xn_ref, acc_ref, *, final_norm):
    f = pl.program_id(1)

    @pl.when(f == 0)
    def _():
        x = h_ref[...]
        y = x * lax.rsqrt(jnp.mean(x * x, axis=-1, keepdims=True) + EPS)
        xn_ref[...] = (y * g_ref[...]).astype(bf16)
        acc_ref[...] = jnp.zeros_like(acc_ref)

    xn = xn_ref[...]
    gate = _dot(xn, wg_ref[...])
    up = _dot(xn, wu_ref[...])
    act = (gate * jax.nn.sigmoid(gate) * up).astype(bf16)
    acc_ref[...] += _dot(act, wd_ref[...])

    @pl.when(f == pl.num_programs(1) - 1)
    def _():
        y = h_ref[...] + acc_ref[...]
        if final_norm:
            y = y * lax.rsqrt(jnp.mean(y * y, axis=-1, keepdims=True) + EPS) * gf_ref[...]
        o_ref[...] = y


def _ffn(h, g, wg, wu, wd, gf, tm, tf, final_norm):
    S, D = h.shape
    F = wg.shape[1]
    return pl.pallas_call(
        functools.partial(_ffn_kernel, final_norm=final_norm),
        grid=(S // tm, F // tf),
        in_specs=[
            pl.BlockSpec((tm, D), lambda i, f: (i, 0)),
            pl.BlockSpec((1, D), lambda i, f: (0, 0)),
            pl.BlockSpec((D, tf), lambda i, f: (0, f)),
            pl.BlockSpec((D, tf), lambda i, f: (0, f)),
            pl.BlockSpec((tf, D), lambda i, f: (f, 0)),
            pl.BlockSpec((1, D), lambda i, f: (0, 0)),
        ],
        out_specs=pl.BlockSpec((tm, D), lambda i, f: (i, 0)),
        out_shape=jax.ShapeDtypeStruct((S, D), f32),
        scratch_shapes=[pltpu.VMEM((tm, D), bf16), pltpu.VMEM((tm, D), f32)],
        compiler_params=_cparams("parallel", "arbitrary"),
        name="ffn",
    )(h, g, wg, wu, wd, gf)


def _rope_tables(pos):
    inv = ROPE_THETA ** (-jnp.arange(0, ROPE_DIM, 2, dtype=f32) / ROPE_DIM)
    ang = pos.astype(f32)[:, None] * inv
    cos, sin = jnp.cos(ang), jnp.sin(ang)
    n = pos.shape[0]
    rest = HEAD_DIM - ROPE_DIM
    cosf = jnp.concatenate([cos, cos, jnp.ones((n, rest), f32)], axis=1)
    sina = jnp.concatenate([-sin, jnp.zeros((n, HEAD_DIM - ROPE_HALF), f32)], axis=1)
    sinb = jnp.concatenate([jnp.zeros((n, ROPE_HALF), f32), sin, jnp.zeros((n, rest), f32)], axis=1)
    return cosf, sina, sinb


def _importance_matrix(n_pad):
    ratio = SLC_BLOCK // CMP_STRIDE
    nb = CMP_BLOCK // CMP_STRIDE
    m = np.zeros((n_pad, LANES), np.float32)
    for n in range(n_pad):
        for k in range(nb):
            j = (n + k) // ratio
            if j < LANES:
                m[n, j] += 1.0
    return m


def kernel(x, positions, norm_mix, w_in, b_gate, cmp_pos, w_cmp_k1, w_cmp_k2, w_cmp_v1, w_cmp_v2,
           norm_grp, w_out, norm_ffn, w_gate, w_up, w_down, norm_final):
    B, S, D = x.shape
    assert B == 1 and S % KV_BLK == 0 and S // SLC_BLOCK <= LANES
    depth = w_in.shape[0]
    n_slc = S // SLC_BLOCK
    n_pad = S // CMP_STRIDE
    win_span = WINDOW + Q_BLK
    tm = 512
    pos = positions[0]

    cosf, sina, sinb = (jnp.tile(t, (1, PROJ_TN // HEAD_DIM)) for t in _rope_tables(pos))
    cmp_end = jnp.minimum(jnp.arange(n_pad) * CMP_STRIDE + CMP_BLOCK - 1, S - 1)
    ccosf, csina, csinb = _rope_tables(pos[cmp_end])

    m_mat = jnp.asarray(_importance_matrix(n_pad), bf16)
    e_np = (np.arange(S)[:, None] // SLC_BLOCK == np.arange(LANES)[None, :])
    e_mat = jnp.asarray(e_np, bf16)
    u_np = np.arange(SB_KV_BLK)[:, None] > np.arange(SB_KV_BLK)[None, :]
    u_mat = jnp.asarray(u_np, bf16)

    q_end = N_HEADS_NSA * HEAD_DIM + 6 * N_KV_NSA * HEAD_DIM
    n_gate = N_BRANCH * N_HEADS_NSA
    w_main = jnp.concatenate([w_in[:, :, :q_end], w_in[:, :, q_end + n_gate:]], axis=2).astype(bf16)
    w_g = jnp.pad(w_in[:, :, q_end:q_end + n_gate], ((0, 0), (0, 0), (0, LANES - n_gate))).astype(bf16)
    b_g = jnp.pad(b_gate, ((0, 0), (0, LANES - n_gate)))[:, None, :]
    w1 = jnp.stack([w_cmp_k1, w_cmp_v1], axis=1).astype(bf16)
    w2 = jnp.stack([w_cmp_k2, w_cmp_v2], axis=1).astype(bf16)
    cp = jnp.broadcast_to(cmp_pos.reshape(depth, 1, CMP_BLOCK * HEAD_DIM), (depth, 8, CMP_BLOCK * HEAD_DIM)).astype(bf16)
    w_out_b = w_out.astype(bf16)
    w_gate_b, w_up_b, w_down_b = w_gate.astype(bf16), w_up.astype(bf16), w_down.astype(bf16)

    h = x[0]
    for layer in range(depth):
        P, gates = _proj(h, norm_mix[layer][None, :], w_main[layer], w_g[layer], b_g[layer],
                         cosf, sina, sinb, tm)
        n_c = 2 * N_KV_NSA
        t2 = P[:, COL_KC * HEAD_DIM:(COL_KC + n_c) * HEAD_DIM].reshape(S, n_c, HEAD_DIM)
        t2 = t2.transpose(1, 0, 2).reshape(n_c, n_pad, CMP_STRIDE * HEAD_DIM)
        kvc = _compress(t2, w1[layer], w2[layer], cp[layer], ccosf, csina, csinb)
        o_cmp, selb = _cmp_topk(P, kvc, m_mat, n_slc)
        gg = norm_grp[layer][None, :]
        o_nsa = _nsa(P, e_mat, selb, o_cmp, gates, gg, KV_BLK, win_span)
        o_sb = _sb(P, u_mat, gg)
        h = _out_proj(h, o_nsa, o_sb, w_out_b[layer], tm, 512)
        h = _ffn(h, norm_ffn[layer][None, :], w_gate_b[layer], w_up_b[layer], w_down_b[layer],
                 norm_final[None, :], tm, 512, layer == depth - 1)
    return h[None]
```

```python
import functools

import jax
import jax.numpy as jnp
import numpy as np
from jax import lax
from jax.experimental import pallas as pl
from jax.experimental.pallas import tpu as pltpu

HEAD_DIM = 128
N_HEADS_NSA = 8
N_KV_NSA = 2
GROUP = N_HEADS_NSA // N_KV_NSA
N_HEADS_SB = 8
ROPE_DIM = HEAD_DIM // 4
ROPE_HALF = ROPE_DIM // 2
ROPE_THETA = 500000.0
CMP_BLOCK = 32
CMP_STRIDE = 16
CMP_HIDDEN = 256
SLC_BLOCK = 64
SLC_SHIFT = SLC_BLOCK.bit_length() - 1
SLC_TOP_N = 16
WINDOW = 512
N_BRANCH = 3
EPS = 1e-6
NEG_INF = -1e30
FORCE_SCORE = 1e4
NEG_SCORE = -1e4
SCALE = HEAD_DIM ** -0.5
LOG2E = 1.4426950408889634

LANES = 128
SUBLANES = 8
SEL_MASK_BIAS = -30000.0
SB_UNDERFLOW = -105.0
VMEM_LIMIT = 52 * 1024 * 1024

Q_BLK = 128
KV_BLK = 512
SB_KV_BLK = 128
SB_SPAN = 3
SB_HEADS = 4

COL_Q = 0
COL_KC = 8
COL_KS, COL_VS, COL_KW, COL_VW = 12, 14, 16, 18
COL_SBQ, COL_SBK, COL_SBV = 20, 28, 36
N_MAIN = 44 * HEAD_DIM

f32 = jnp.float32
bf16 = jnp.bfloat16


def _cparams(*sem):
    return pltpu.CompilerParams(dimension_semantics=sem, vmem_limit_bytes=VMEM_LIMIT)


def _dot(a, b):
    return jnp.dot(a, b, preferred_element_type=f32)


def _dot_nt(a, b):
    return lax.dot_general(a, b, (((1,), (1,)), ((), ())), preferred_element_type=f32)


def _split_dot(a, b):
    hi = a.astype(bf16)
    lo = (a - hi.astype(f32)).astype(bf16)
    return _dot(hi, b) + _dot(lo, b)


def _rope(x, cosf, sina, sinb):
    reps = x.shape[-1] // HEAD_DIM
    w = x.shape[-1]
    cosf, sina, sinb = (jnp.concatenate([t] * reps, axis=1) if reps > 1 else t for t in (cosf, sina, sinb))
    return (x * cosf + pltpu.roll(x, w - ROPE_HALF, axis=1) * sina
            + pltpu.roll(x, ROPE_HALF, axis=1) * sinb)


def _rms(x):
    return x * lax.rsqrt(jnp.mean(x * x, axis=-1, keepdims=True) + EPS)


def _norm_kernel(h_ref, g_ref, o_ref):
    o_ref[...] = (_rms(h_ref[...]) * g_ref[...]).astype(bf16)


def _norm_gate_kernel(h_ref, g_ref, wg_ref, bg_ref, o_ref, gate_ref):
    xn = (_rms(h_ref[...]) * g_ref[...]).astype(bf16)
    o_ref[...] = xn
    gate_ref[...] = jax.nn.sigmoid(_dot(xn, wg_ref[...]) + bg_ref[...])


def _norm(h, g, tm):
    S, D = h.shape
    return pl.pallas_call(
        _norm_kernel,
        grid=(S // tm,),
        in_specs=[pl.BlockSpec((tm, D), lambda i: (i, 0)), pl.BlockSpec((1, D), lambda i: (0, 0))],
        out_specs=pl.BlockSpec((tm, D), lambda i: (i, 0)),
        out_shape=jax.ShapeDtypeStruct((S, D), bf16),
        compiler_params=_cparams("parallel"),
        name="norm",
    )(h, g)


def _norm_gate(h, g, wg, bg, tm):
    S, D = h.shape
    return pl.pallas_call(
        _norm_gate_kernel,
        grid=(S // tm,),
        in_specs=[
            pl.BlockSpec((tm, D), lambda i: (i, 0)),
            pl.BlockSpec((1, D), lambda i: (0, 0)),
            pl.BlockSpec((D, LANES), lambda i: (0, 0)),
            pl.BlockSpec((1, LANES), lambda i: (0, 0)),
        ],
        out_specs=[pl.BlockSpec((tm, D), lambda i: (i, 0)), pl.BlockSpec((tm, LANES), lambda i: (i, 0))],
        out_shape=[jax.ShapeDtypeStruct((S, D), bf16), jax.ShapeDtypeStruct((S, LANES), f32)],
        compiler_params=_cparams("parallel"),
        name="norm_gate",
    )(h, g, wg, bg)


PROJ_TN = 512
PROJ_HEADS = PROJ_TN // HEAD_DIM


def _proj_kernel(x_ref, w_ref, cosf_ref, sina_ref, sinb_ref, o_ref):
    j = pl.program_id(1)
    acc = _dot(x_ref[...], w_ref[...])
    tables = (cosf_ref[...], sina_ref[...], sinb_ref[...])
    q_tiles = N_HEADS_NSA // PROJ_HEADS
    k_width = N_KV_NSA * HEAD_DIM
    is_q = j < q_tiles
    is_kv = (j == COL_KS // PROJ_HEADS) | (j == COL_KW // PROJ_HEADS)
    is_sbq = (j >= COL_SBQ // PROJ_HEADS) & (j < COL_SBK // PROJ_HEADS)

    @pl.when(is_q)
    def _():
        o_ref[...] = (_rope(acc, *tables) * (SCALE * LOG2E)).astype(bf16)

    @pl.when(is_kv)
    def _():
        o_ref[:, :k_width] = _rope(acc[:, :k_width], *tables).astype(bf16)
        o_ref[:, k_width:] = acc[:, k_width:].astype(bf16)

    @pl.when(is_sbq)
    def _():
        o_ref[...] = (acc * SCALE).astype(bf16)

    @pl.when(jnp.logical_not(is_q | is_kv | is_sbq))
    def _():
        o_ref[...] = acc.astype(bf16)


def _proj(xn, w, cosf, sina, sinb, tm):
    S, D = xn.shape
    N = w.shape[1]
    tn = PROJ_TN
    assert COL_VS == COL_KS + N_KV_NSA and COL_KS % PROJ_HEADS == 0 and COL_KW % PROJ_HEADS == 0
    table = pl.BlockSpec((tm, HEAD_DIM), lambda i, j: (i, 0))
    return pl.pallas_call(
        _proj_kernel,
        grid=(S // tm, N // tn),
        in_specs=[
            pl.BlockSpec((tm, D), lambda i, j: (i, 0)),
            pl.BlockSpec((D, tn), lambda i, j: (0, j)),
            table, table, table,
        ],
        out_specs=pl.BlockSpec((tm, tn), lambda i, j: (i, j)),
        out_shape=jax.ShapeDtypeStruct((S, N), bf16),
        compiler_params=_cparams("parallel", "parallel"),
        name="proj",
    )(xn, w, cosf, sina, sinb)


def _gelu_tanh(x):
    return 0.5 * x * (1.0 + jnp.tanh(np.sqrt(2.0 / np.pi) * (x + 0.044715 * (x * x * x))))


def _compress_kernel(t_ref, w1_ref, w2_ref, cp_ref, cosf_ref, sina_ref, sinb_ref, o_ref):
    i = pl.program_id(0)
    t = t_ref[0]
    half = CMP_STRIDE * HEAD_DIM
    a = _dot(t, w1_ref[0, :half, :])
    b = _dot(t, w1_ref[0, half:, :])
    n_pad = a.shape[0]
    bias = _dot(cp_ref[...], w1_ref[0])[0:1, :]
    hid = a + pltpu.roll(b, n_pad - 1, axis=0) + bias
    c = _dot(_gelu_tanh(hid).astype(bf16), w2_ref[0])

    @pl.when(i < N_KV_NSA)
    def _():
        o_ref[0] = _rope(c, cosf_ref[...], sina_ref[...], sinb_ref[...]).astype(bf16)

    @pl.when(i >= N_KV_NSA)
    def _():
        o_ref[0] = c.astype(bf16)


def _compress(t2, w1, w2, cp, ccosf, csina, csinb):
    n4, n_pad, width = t2.shape
    return pl.pallas_call(
        _compress_kernel,
        grid=(n4,),
        in_specs=[
            pl.BlockSpec((1, n_pad, width), lambda i: (i, 0, 0)),
            pl.BlockSpec((1, 2 * width, CMP_HIDDEN), lambda i: (i // N_KV_NSA, 0, 0)),
            pl.BlockSpec((1, CMP_HIDDEN, HEAD_DIM), lambda i: (i // N_KV_NSA, 0, 0)),
            pl.BlockSpec((SUBLANES, 2 * width), lambda i: (0, 0)),
            pl.BlockSpec((n_pad, HEAD_DIM), lambda i: (0, 0)),
            pl.BlockSpec((n_pad, HEAD_DIM), lambda i: (0, 0)),
            pl.BlockSpec((n_pad, HEAD_DIM), lambda i: (0, 0)),
        ],
        out_specs=pl.BlockSpec((1, n_pad, HEAD_DIM), lambda i: (i, 0, 0)),
        out_shape=jax.ShapeDtypeStruct((n4, n_pad, HEAD_DIM), bf16),
        compiler_params=_cparams("parallel"),
        name="compress",
    )(t2, w1, w2, cp, ccosf, csina, csinb)


def _cmp_topk_kernel(q_ref, kc_ref, vc_ref, m_ref, o_ref, selb_ref, key_ref, *, n_slc):
    i = pl.program_id(1)
    t0 = i * Q_BLK
    rows = GROUP * Q_BLK
    kc = kc_ref[0]
    vc = vc_ref[0]
    n_pad = kc.shape[0]
    q4 = jnp.concatenate([q_ref[:, r * HEAD_DIM:(r + 1) * HEAD_DIM] for r in range(GROUP)], axis=0)
    s = _dot_nt(q4, kc)
    t = t0 + (lax.broadcasted_iota(jnp.int32, (rows, n_pad), 0) & (Q_BLK - 1))
    cmp_end = lax.broadcasted_iota(jnp.int32, (rows, n_pad), 1) * CMP_STRIDE + (CMP_BLOCK - 1)
    mask = cmp_end <= t
    s = jnp.where(mask, s, NEG_INF)
    m = jnp.max(s, axis=-1, keepdims=True)
    p = jnp.where(mask, jnp.exp2(s - m), 0.0)
    l = jnp.sum(p, axis=-1, keepdims=True)
    p = p / jnp.where(l > 0.0, l, 1.0)
    o = _dot(p.astype(bf16), vc)
    for r in range(GROUP):
        o_ref[:, r * HEAD_DIM:(r + 1) * HEAD_DIM] = o[r * Q_BLK:(r + 1) * Q_BLK, :]

    p_imp = p[0:Q_BLK]
    for r in range(1, GROUP):
        p_imp = p_imp + p[r * Q_BLK:(r + 1) * Q_BLK]
    p_slc_t = _split_dot(p_imp, m_ref[...]).T

    blk = lax.broadcasted_iota(jnp.int32, (LANES, Q_BLK), 0)
    blk_t = (t0 + lax.broadcasted_iota(jnp.int32, (LANES, Q_BLK), 1)) >> SLC_SHIFT
    causal = blk <= blk_t
    forced = (blk == 0) | (blk == blk_t) | (blk == blk_t - 1)
    imp = jnp.where(causal, jnp.where(forced, FORCE_SCORE, p_slc_t), NEG_SCORE)
    key = jnp.where(causal, pltpu.bitcast(imp, jnp.int32), -1)
    key_ref[...] = key
    key_m1 = key - 1

    def body(b2, rank):
        for u in range(2):
            b = b2 * 2 + u
            row = key_ref[pl.ds(b, 1), :]
            ahead = row > jnp.where(blk > b, key_m1, key)
            rank = rank + jnp.where(ahead, 1.0, 0.0)
        return rank

    n_live = jnp.minimum((t0 + Q_BLK - 1) // SLC_BLOCK + 1, n_slc)
    rank = lax.fori_loop(0, (n_live + 1) // 2, body, jnp.zeros((LANES, Q_BLK), f32))
    sel = (rank < float(min(SLC_TOP_N, n_slc))) & causal
    selb_ref[0] = jnp.where(sel, 0.0, SEL_MASK_BIAS).T.astype(bf16)


def _cmp_topk(P, kvc, m_mat, n_slc):
    S = P.shape[0]
    n_pad = kvc.shape[1]
    gw = GROUP * HEAD_DIM
    return pl.pallas_call(
        functools.partial(_cmp_topk_kernel, n_slc=n_slc),
        grid=(N_KV_NSA, S // Q_BLK),
        in_specs=[
            pl.BlockSpec((Q_BLK, gw), lambda g, i: (i, g)),
            pl.BlockSpec((1, n_pad, HEAD_DIM), lambda g, i: (g, 0, 0)),
            pl.BlockSpec((1, n_pad, HEAD_DIM), lambda g, i: (N_KV_NSA + g, 0, 0)),
            pl.BlockSpec((n_pad, LANES), lambda g, i: (0, 0)),
        ],
        out_specs=[
            pl.BlockSpec((Q_BLK, gw), lambda g, i: (i, g)),
            pl.BlockSpec((1, Q_BLK, LANES), lambda g, i: (g, i, 0)),
        ],
        out_shape=[
            jax.ShapeDtypeStruct((S, N_HEADS_NSA * HEAD_DIM), f32),
            jax.ShapeDtypeStruct((N_KV_NSA, S, LANES), bf16),
        ],
        scratch_shapes=[pltpu.VMEM((LANES, Q_BLK), jnp.int32)],
        compiler_params=_cparams("parallel", "parallel"),
        name="cmp_topk",
    )(P, kvc, kvc, m_mat)


def _nsa_kernel(q_ref, ks_ref, vs_ref, kw_ref, vw_ref, e_ref, selb_ref, ocmp_ref, gate_ref,
                gg_ref, o_ref, *, kv_blk, win_span):
    i = pl.program_id(1)
    t0 = i * Q_BLK
    rows = GROUP * Q_BLK
    S = ks_ref.shape[0]
    q4 = jnp.concatenate([q_ref[:, r * HEAD_DIM:(r + 1) * HEAD_DIM] for r in range(GROUP)], axis=0)
    selb = selb_ref[0]
    q_aug = jnp.concatenate([q4, jnp.concatenate([selb] * GROUP, axis=0)], axis=1)
    half = kv_blk // 2
    ones = jnp.ones((half, HEAD_DIM), bf16)

    def tile(kb, carry, causal_mask):
        m, acc = carry
        s = []
        for hf in range(2):
            k0 = pl.multiple_of(kb * kv_blk + hf * half, half)
            k_aug = jnp.concatenate([ks_ref[pl.ds(k0, half), :], e_ref[pl.ds(k0, half), :]], axis=1)
            s_h = _dot_nt(q_aug, k_aug)
            if causal_mask:
                t = t0 + (lax.broadcasted_iota(jnp.int32, (rows, half), 0) & (Q_BLK - 1))
                s_h = jnp.where(k0 + lax.broadcasted_iota(jnp.int32, (rows, half), 1) <= t, s_h, NEG_INF)
            s.append(s_h)
        for hf in range(2):
            k0 = pl.multiple_of(kb * kv_blk + hf * half, half)
            m_new = jnp.maximum(m, jnp.max(s[hf], axis=-1, keepdims=True))
            p = jnp.exp2(s[hf] - m_new).astype(bf16)
            v_aug = jnp.concatenate([vs_ref[pl.ds(k0, half), :], ones], axis=1)
            acc = jnp.exp2(m - m_new) * acc + _dot(p, v_aug)
            m = m_new
        return m, acc

    kd = t0 // kv_blk
    init = (jnp.full((rows, 1), NEG_INF, f32), jnp.zeros((rows, 2 * HEAD_DIM), f32))
    carry = lax.fori_loop(0, kd, lambda kb, c: tile(kb, c, False), init)
    _, acc = tile(kd, carry, True)
    o_slc = acc[:, :HEAD_DIM] / acc[:, HEAD_DIM:]

    w0 = pl.multiple_of(jnp.clip(t0 - WINDOW, 0, S - win_span), Q_BLK)
    s = _dot_nt(q4, kw_ref[pl.ds(w0, win_span), :])
    t = t0 + (lax.broadcasted_iota(jnp.int32, (rows, win_span), 0) & (Q_BLK - 1))
    diff = t - (w0 + lax.broadcasted_iota(jnp.int32, (rows, win_span), 1))
    s = jnp.where((diff >= 0) & (diff < WINDOW), s, NEG_INF)
    p = jnp.exp2(s - jnp.max(s, axis=-1, keepdims=True))
    l_w = jnp.sum(p, axis=-1, keepdims=True)
    o_win = _dot(p.astype(bf16), vw_ref[pl.ds(w0, win_span), :]) / l_w

    gates = gate_ref[...]
    g = pl.program_id(0)
    for r in range(GROUP):
        rs = slice(r * Q_BLK, (r + 1) * Q_BLK)
        cs = slice(r * HEAD_DIM, (r + 1) * HEAD_DIM)
        h_lane = lax.broadcasted_iota(jnp.int32, gates.shape, 1) - (g * GROUP + r)

        def gate_col(branch):
            return jnp.sum(jnp.where(h_lane == branch * N_HEADS_NSA, gates, 0.0), axis=-1, keepdims=True)

        o = gate_col(0) * ocmp_ref[:, cs] + gate_col(1) * o_slc[rs] + gate_col(2) * o_win[rs]
        o_ref[:, cs] = (_rms(o) * gg_ref[:, cs]).astype(bf16)


def _nsa(P, e_mat, selb, o_cmp, gates, gg, kv_blk, win_span):
    S = P.shape[0]
    gw = GROUP * HEAD_DIM
    full = lambda c: pl.BlockSpec((S, HEAD_DIM), lambda g, i, c=c: (0, c + g))
    return pl.pallas_call(
        functools.partial(_nsa_kernel, kv_blk=kv_blk, win_span=win_span),
        grid=(N_KV_NSA, S // Q_BLK),
        in_specs=[
            pl.BlockSpec((Q_BLK, gw), lambda g, i: (i, g)),
            full(COL_KS), full(COL_VS), full(COL_KW), full(COL_VW),
            pl.BlockSpec((S, LANES), lambda g, i: (0, 0)),
            pl.BlockSpec((1, Q_BLK, LANES), lambda g, i: (g, i, 0)),
            pl.BlockSpec((Q_BLK, gw), lambda g, i: (i, g)),
            pl.BlockSpec((Q_BLK, LANES), lambda g, i: (i, 0)),
            pl.BlockSpec((1, gw), lambda g, i: (0, g)),
        ],
        out_specs=pl.BlockSpec((Q_BLK, gw), lambda g, i: (i, g)),
        out_shape=jax.ShapeDtypeStruct((S, N_HEADS_NSA * HEAD_DIM), bf16),
        compiler_params=_cparams("parallel", "parallel"),
        name="nsa",
    )(P, P, P, P, P, e_mat, selb, o_cmp, gates, gg)


def _sb_span(q, k, v, u, t0, k0, carry):
    n = k.shape[0] // SB_KV_BLK
    z = _dot_nt(q, k)
    t = t0 + lax.broadcasted_iota(jnp.int32, z.shape, 0)
    valid = (k0 + lax.broadcasted_iota(jnp.int32, z.shape, 1)) < t
    log_1m = jnp.where(valid, -(jnp.maximum(z, 0.0) + jnp.log1p(jnp.exp(-jnp.abs(z)))), 0.0)
    after = [None] * n
    for c in reversed(range(n)):
        part = log_1m[:, c * SB_KV_BLK:(c + 1) * SB_KV_BLK]
        after[c] = _split_dot(part, u) + carry
        carry = carry + jnp.sum(part, axis=-1, keepdims=True)
    after = jnp.concatenate(after, axis=1) if n > 1 else after[0]
    a = jnp.where(valid, jnp.exp(z + log_1m + after), 0.0)
    return _dot(a.astype(bf16), v), carry


def _sb_kernel(q_ref, k_ref, v_ref, u_ref, gg_ref, o_ref):
    i = pl.program_id(1)
    t0 = i * Q_BLK
    u = u_ref[...]
    span = SB_SPAN * SB_KV_BLK
    heads = [slice(h * HEAD_DIM, (h + 1) * HEAD_DIM) for h in range(SB_HEADS)]
    qs = [q_ref[:, hs] for hs in heads]

    s0 = pl.multiple_of(jnp.maximum(t0 - (SB_SPAN - 1) * SB_KV_BLK, 0), SB_KV_BLK)
    zero = jnp.zeros((Q_BLK, 1), f32)
    accs, carries = [], []
    for h, hs in enumerate(heads):
        acc, carry = _sb_span(qs[h], k_ref[pl.ds(s0, span), hs], v_ref[pl.ds(s0, span), hs], u, t0, s0, zero)
        accs.append(acc)
        carries.append(carry)

    def carry_max(cs):
        return jnp.max(functools.reduce(jnp.maximum, cs))

    def cond(c):
        nt, cmax, _, _ = c
        return (nt >= 0) & (cmax > SB_UNDERFLOW)

    def body(c):
        nt, _, cs, acs = c
        k0 = pl.multiple_of(nt * SB_KV_BLK, SB_KV_BLK)
        new_cs, new_acs = [], []
        for h, hs in enumerate(heads):
            acc, carry = _sb_span(qs[h], k_ref[pl.ds(k0, SB_KV_BLK), hs], v_ref[pl.ds(k0, SB_KV_BLK), hs],
                                  u, t0, k0, cs[h])
            new_cs.append(carry)
            new_acs.append(acs[h] + acc)
        return nt - 1, carry_max(new_cs), tuple(new_cs), tuple(new_acs)

    init = (s0 // SB_KV_BLK - 1, carry_max(carries), tuple(carries), tuple(accs))
    _, _, _, accs = lax.while_loop(cond, body, init)
    for h, hs in enumerate(heads):
        o_ref[:, hs] = (_rms(accs[h]) * gg_ref[:, hs]).astype(bf16)


def _sb(P, u_mat, gg):
    S = P.shape[0]
    hw = SB_HEADS * HEAD_DIM
    return pl.pallas_call(
        _sb_kernel,
        grid=(N_HEADS_SB // SB_HEADS, S // Q_BLK),
        in_specs=[
            pl.BlockSpec((Q_BLK, hw), lambda h, i: (i, COL_SBQ // SB_HEADS + h)),
            pl.BlockSpec((S, hw), lambda h, i: (0, COL_SBK // SB_HEADS + h)),
            pl.BlockSpec((S, hw), lambda h, i: (0, COL_SBV // SB_HEADS + h)),
            pl.BlockSpec((SB_KV_BLK, SB_KV_BLK), lambda h, i: (0, 0)),
            pl.BlockSpec((1, hw), lambda h, i: (0, N_HEADS_NSA // SB_HEADS + h)),
        ],
        out_specs=pl.BlockSpec((Q_BLK, hw), lambda h, i: (i, h)),
        out_shape=jax.ShapeDtypeStruct((S, N_HEADS_SB * HEAD_DIM), bf16),
        compiler_params=_cparams("parallel", "parallel"),
        name="sb",
    )(P, P, P, u_mat, gg)


def _out_proj_kernel(h_ref, a_ref, b_ref, wa_ref, wb_ref, o_ref):
    o_ref[...] = h_ref[...] + _dot(a_ref[...], wa_ref[...]) + _dot(b_ref[...], wb_ref[...])


def _out_proj(h, o_nsa, o_sb, w_out, tm, tn):
    S, D = h.shape
    ka = o_nsa.shape[1]
    kb = o_sb.shape[1]
    return pl.pallas_call(
        _out_proj_kernel,
        grid=(S // tm, D // tn),
        in_specs=[
            pl.BlockSpec((tm, tn), lambda i, j: (i, j)),
            pl.BlockSpec((tm, ka), lambda i, j: (i, 0)),
            pl.BlockSpec((tm, kb), lambda i, j: (i, 0)),
            pl.BlockSpec((ka, tn), lambda i, j: (0, j)),
            pl.BlockSpec((kb, tn), lambda i, j: (ka // kb, j)),
        ],
        out_specs=pl.BlockSpec((tm, tn), lambda i, j: (i, j)),
        out_shape=jax.ShapeDtypeStruct((S, D), f32),
        compiler_params=_cparams("parallel", "parallel"),
        name="out_proj",
    )(h, o_nsa, o_sb, w_out, w_out)


def _ffn_kernel(xn_ref, h_ref, wg_ref, wu_ref, wd_ref, gf_ref, o_ref, *, final_norm):
    f = pl.program_id(1)

    @pl.when(f == 0)
    def _():
        o_ref[...] = h_ref[...]

    xn = xn_ref[...]
    gate = _dot(xn, wg_ref[...])
    up = _dot(xn, wu_ref[...])
    act = (gate * jax.nn.sigmoid(gate) * up).astype(bf16)
    o_ref[...] += _dot(act, wd_ref[...])

    if final_norm:
        @pl.when(f == pl.num_programs(1) - 1)
        def _():
            o_ref[...] = _rms(o_ref[...]) * gf_ref[...]


def _ffn(xn, h, wg, wu, wd, gf, tm, tf, final_norm):
    S, D = h.shape
    F = wg.shape[1]
    return pl.pallas_call(
        functools.partial(_ffn_kernel, final_norm=final_norm),
        grid=(S // tm, F // tf),
        in_specs=[
            pl.BlockSpec((tm, D), lambda i, f: (i, 0)),
            pl.BlockSpec((tm, D), lambda i, f: (i, 0)),
            pl.BlockSpec((D, tf), lambda i, f: (0, f)),
            pl.BlockSpec((D, tf), lambda i, f: (0, f)),
            pl.BlockSpec((tf, D), lambda i, f: (f, 0)),
            pl.BlockSpec((1, D), lambda i, f: (0, 0)),
        ],
        out_specs=pl.BlockSpec((tm, D), lambda i, f: (i, 0)),
        out_shape=jax.ShapeDtypeStruct((S, D), f32),
        compiler_params=_cparams("parallel", "arbitrary"),
        name="ffn",
    )(xn, h, wg, wu, wd, gf)


def _rope_tables(pos):
    inv = ROPE_THETA ** (-jnp.arange(0, ROPE_DIM, 2, dtype=f32) / ROPE_DIM)
    ang = pos.astype(f32)[:, None] * inv
    cos, sin = jnp.cos(ang), jnp.sin(ang)
    n = pos.shape[0]
    rest = HEAD_DIM - ROPE_DIM
    cosf = jnp.concatenate([cos, cos, jnp.ones((n, rest), f32)], axis=1)
    sina = jnp.concatenate([-sin, jnp.zeros((n, HEAD_DIM - ROPE_HALF), f32)], axis=1)
    sinb = jnp.concatenate([jnp.zeros((n, ROPE_HALF), f32), sin, jnp.zeros((n, rest), f32)], axis=1)
    return cosf, sina, sinb


def _importance_matrix(n_pad):
    ratio = SLC_BLOCK // CMP_STRIDE
    nb = CMP_BLOCK // CMP_STRIDE
    m = np.zeros((n_pad, LANES), np.float32)
    for n in range(n_pad):
        for k in range(nb):
            j = (n + k) // ratio
            if j < LANES:
                m[n, j] += 1.0
    return m


def kernel(x, positions, norm_mix, w_in, b_gate, cmp_pos, w_cmp_k1, w_cmp_k2, w_cmp_v1, w_cmp_v2,
           norm_grp, w_out, norm_ffn, w_gate, w_up, w_down, norm_final):
    B, S, D = x.shape
    assert B == 1 and S % KV_BLK == 0 and S // SLC_BLOCK <= LANES
    depth = w_in.shape[0]
    n_slc = S // SLC_BLOCK
    n_pad = S // CMP_STRIDE
    win_span = WINDOW + Q_BLK
    tm = 512
    tm_proj = 1024 if S % 1024 == 0 else tm
    pos = positions[0]

    cosf, sina, sinb = _rope_tables(pos)
    cmp_end = jnp.minimum(jnp.arange(n_pad) * CMP_STRIDE + CMP_BLOCK - 1, S - 1)
    ccosf, csina, csinb = _rope_tables(pos[cmp_end])

    m_mat = jnp.asarray(_importance_matrix(n_pad), bf16)
    e_np = (np.arange(S)[:, None] // SLC_BLOCK == np.arange(LANES)[None, :])
    e_mat = jnp.asarray(e_np, bf16)
    u_np = np.arange(SB_KV_BLK)[:, None] > np.arange(SB_KV_BLK)[None, :]
    u_mat = jnp.asarray(u_np, bf16)

    q_end = N_HEADS_NSA * HEAD_DIM + 6 * N_KV_NSA * HEAD_DIM
    n_gate = N_BRANCH * N_HEADS_NSA
    w_main = jnp.concatenate([w_in[:, :, :q_end], w_in[:, :, q_end + n_gate:]], axis=2).astype(bf16)
    w_g = jnp.pad(w_in[:, :, q_end:q_end + n_gate], ((0, 0), (0, 0), (0, LANES - n_gate))).astype(bf16)
    b_g = jnp.pad(b_gate, ((0, 0), (0, LANES - n_gate)))[:, None, :]
    w1 = jnp.stack([w_cmp_k1, w_cmp_v1], axis=1).astype(bf16)
    w2 = jnp.stack([w_cmp_k2, w_cmp_v2], axis=1).astype(bf16)
    cp = jnp.broadcast_to(cmp_pos.reshape(depth, 1, CMP_BLOCK * HEAD_DIM),
                          (depth, SUBLANES, CMP_BLOCK * HEAD_DIM)).astype(bf16)
    w_out_b = w_out.astype(bf16)
    w_gate_b, w_up_b, w_down_b = w_gate.astype(bf16), w_up.astype(bf16), w_down.astype(bf16)

    h = x[0]
    for layer in range(depth):
        xn, gates = _norm_gate(h, norm_mix[layer][None, :], w_g[layer], b_g[layer], tm)
        P = _proj(xn, w_main[layer], cosf, sina, sinb, tm_proj)
        n_c = 2 * N_KV_NSA
        t2 = P[:, COL_KC * HEAD_DIM:(COL_KC + n_c) * HEAD_DIM].reshape(S, n_c, HEAD_DIM)
        t2 = t2.transpose(1, 0, 2).reshape(n_c, n_pad, CMP_STRIDE * HEAD_DIM)
        kvc = _compress(t2, w1[layer], w2[layer], cp[layer], ccosf, csina, csinb)
        o_cmp, selb = _cmp_topk(P, kvc, m_mat, n_slc)
        gg = norm_grp[layer][None, :]
        o_nsa = _nsa(P, e_mat, selb, o_cmp, gates, gg, KV_BLK, win_span)
        o_sb = _sb(P, u_mat, gg)
        h = _out_proj(h, o_nsa, o_sb, w_out_b[layer], tm, 512)
        xn = _norm(h, norm_ffn[layer][None, :], tm)
        h = _ffn(xn, h, w_gate_b[layer], w_up_b[layer], w_down_b[layer],
                 norm_final[None, :], tm, 512, layer == depth - 1)
    return h[None]
```

```python
import functools

import jax
import jax.numpy as jnp
import numpy as np
from jax import lax
from jax.experimental import pallas as pl
from jax.experimental.pallas import tpu as pltpu

HEAD_DIM = 128
N_HEADS_NSA = 8
N_KV_NSA = 2
GROUP = N_HEADS_NSA // N_KV_NSA
N_HEADS_SB = 8
ROPE_DIM = HEAD_DIM // 4
ROPE_HALF = ROPE_DIM // 2
ROPE_THETA = 500000.0
CMP_BLOCK = 32
CMP_STRIDE = 16
CMP_HIDDEN = 256
SLC_BLOCK = 64
SLC_SHIFT = SLC_BLOCK.bit_length() - 1
SLC_TOP_N = 16
WINDOW = 512
N_BRANCH = 3
EPS = 1e-6
NEG_INF = -1e30
FORCE_SCORE = 1e4
NEG_SCORE = -1e4
SCALE = HEAD_DIM ** -0.5

LANES = 128
SUBLANES = 8
SEL_MASK_BIAS = -30000.0
SB_UNDERFLOW = -105.0
VMEM_LIMIT = 52 * 1024 * 1024

Q_BLK = 128
NSA_Q_BLK = 256
KV_BLK = 512
SB_KV_BLK = 128
SB_SPAN = 3
SB_LOOP_SPAN = 2
SB_HEADS = 4

COL_Q = 0
COL_KC = 8
COL_KS, COL_VS, COL_KW, COL_VW = 12, 14, 16, 18
COL_SBQ, COL_SBK, COL_SBV = 20, 28, 36
N_MAIN = 44 * HEAD_DIM

f32 = jnp.float32
bf16 = jnp.bfloat16


def _cparams(*sem):
    return pltpu.CompilerParams(dimension_semantics=sem, vmem_limit_bytes=VMEM_LIMIT)


def _dot(a, b):
    return jnp.dot(a, b, preferred_element_type=f32)


def _dot_nt(a, b):
    return lax.dot_general(a, b, (((1,), (1,)), ((), ())), preferred_element_type=f32)


def _split_dot(a, b):
    hi = a.astype(bf16)
    lo = (a - hi.astype(f32)).astype(bf16)
    return _dot(hi, b) + _dot(lo, b)


def _rope(x, cosf, sina, sinb):
    reps = x.shape[-1] // HEAD_DIM
    w = x.shape[-1]
    cosf, sina, sinb = (jnp.concatenate([t] * reps, axis=1) if reps > 1 else t for t in (cosf, sina, sinb))
    return (x * cosf + pltpu.roll(x, w - ROPE_HALF, axis=1) * sina
            + pltpu.roll(x, ROPE_HALF, axis=1) * sinb)


def _rms(x):
    return x * lax.rsqrt(jnp.mean(x * x, axis=-1, keepdims=True) + EPS)


def _norm_kernel(h_ref, g_ref, o_ref):
    o_ref[...] = (_rms(h_ref[...]) * g_ref[...]).astype(bf16)


def _norm_gate_kernel(h_ref, g_ref, wg_ref, bg_ref, o_ref, gate_ref):
    xn = (_rms(h_ref[...]) * g_ref[...]).astype(bf16)
    o_ref[...] = xn
    gate_ref[...] = jax.nn.sigmoid(_dot(xn, wg_ref[...]) + bg_ref[...])


def _norm(h, g, tm):
    S, D = h.shape
    return pl.pallas_call(
        _norm_kernel,
        grid=(S // tm,),
        in_specs=[pl.BlockSpec((tm, D), lambda i: (i, 0)), pl.BlockSpec((1, D), lambda i: (0, 0))],
        out_specs=pl.BlockSpec((tm, D), lambda i: (i, 0)),
        out_shape=jax.ShapeDtypeStruct((S, D), bf16),
        compiler_params=_cparams("parallel"),
        name="norm",
    )(h, g)


def _norm_gate(h, g, wg, bg, tm):
    S, D = h.shape
    return pl.pallas_call(
        _norm_gate_kernel,
        grid=(S // tm,),
        in_specs=[
            pl.BlockSpec((tm, D), lambda i: (i, 0)),
            pl.BlockSpec((1, D), lambda i: (0, 0)),
            pl.BlockSpec((D, LANES), lambda i: (0, 0)),
            pl.BlockSpec((1, LANES), lambda i: (0, 0)),
        ],
        out_specs=[pl.BlockSpec((tm, D), lambda i: (i, 0)), pl.BlockSpec((tm, LANES), lambda i: (i, 0))],
        out_shape=[jax.ShapeDtypeStruct((S, D), bf16), jax.ShapeDtypeStruct((S, LANES), f32)],
        compiler_params=_cparams("parallel"),
        name="norm_gate",
    )(h, g, wg, bg)


PROJ_TN = 512
PROJ_HEADS = PROJ_TN // HEAD_DIM


def _proj_kernel(x_ref, w_ref, cosf_ref, sina_ref, sinb_ref, o_ref):
    j = pl.program_id(1)
    acc = _dot(x_ref[...], w_ref[...])
    tables = (cosf_ref[...], sina_ref[...], sinb_ref[...])
    q_tiles = N_HEADS_NSA // PROJ_HEADS
    k_width = N_KV_NSA * HEAD_DIM
    is_q = j < q_tiles
    is_kv = (j == COL_KS // PROJ_HEADS) | (j == COL_KW // PROJ_HEADS)
    is_sbq = (j >= COL_SBQ // PROJ_HEADS) & (j < COL_SBK // PROJ_HEADS)

    @pl.when(is_q)
    def _():
        o_ref[...] = (_rope(acc, *tables) * SCALE).astype(bf16)

    @pl.when(is_kv)
    def _():
        o_ref[:, :k_width] = _rope(acc[:, :k_width], *tables).astype(bf16)
        o_ref[:, k_width:] = acc[:, k_width:].astype(bf16)

    @pl.when(is_sbq)
    def _():
        o_ref[...] = (acc * SCALE).astype(bf16)

    @pl.when(jnp.logical_not(is_q | is_kv | is_sbq))
    def _():
        o_ref[...] = acc.astype(bf16)


def _proj(xn, w, cosf, sina, sinb, tm):
    S, D = xn.shape
    N = w.shape[1]
    tn = PROJ_TN
    assert COL_VS == COL_KS + N_KV_NSA and COL_KS % PROJ_HEADS == 0 and COL_KW % PROJ_HEADS == 0
    table = pl.BlockSpec((tm, HEAD_DIM), lambda i, j: (i, 0))
    return pl.pallas_call(
        _proj_kernel,
        grid=(S // tm, N // tn),
        in_specs=[
            pl.BlockSpec((tm, D), lambda i, j: (i, 0)),
            pl.BlockSpec((D, tn), lambda i, j: (0, j)),
            table, table, table,
        ],
        out_specs=pl.BlockSpec((tm, tn), lambda i, j: (i, j)),
        out_shape=jax.ShapeDtypeStruct((S, N), bf16),
        compiler_params=_cparams("parallel", "parallel"),
        name="proj",
    )(xn, w, cosf, sina, sinb)


def _gelu_tanh(x):
    return 0.5 * x * (1.0 + jnp.tanh(np.sqrt(2.0 / np.pi) * (x + 0.044715 * (x * x * x))))


def _compress_kernel(t_ref, w1_ref, w2_ref, cp_ref, cosf_ref, sina_ref, sinb_ref, o_ref):
    i = pl.program_id(0)
    t = t_ref[0]
    half = CMP_STRIDE * HEAD_DIM
    a = _dot(t, w1_ref[0, :half, :])
    b = _dot(t, w1_ref[0, half:, :])
    n_pad = a.shape[0]
    bias = _dot(cp_ref[...], w1_ref[0])[0:1, :]
    hid = a + pltpu.roll(b, n_pad - 1, axis=0) + bias
    c = _dot(_gelu_tanh(hid).astype(bf16), w2_ref[0])

    @pl.when(i < N_KV_NSA)
    def _():
        o_ref[0] = _rope(c, cosf_ref[...], sina_ref[...], sinb_ref[...]).astype(bf16)

    @pl.when(i >= N_KV_NSA)
    def _():
        o_ref[0] = c.astype(bf16)


def _compress(t2, w1, w2, cp, ccosf, csina, csinb):
    n4, n_pad, width = t2.shape
    return pl.pallas_call(
        _compress_kernel,
        grid=(n4,),
        in_specs=[
            pl.BlockSpec((1, n_pad, width), lambda i: (i, 0, 0)),
            pl.BlockSpec((1, 2 * width, CMP_HIDDEN), lambda i: (i // N_KV_NSA, 0, 0)),
            pl.BlockSpec((1, CMP_HIDDEN, HEAD_DIM), lambda i: (i // N_KV_NSA, 0, 0)),
            pl.BlockSpec((SUBLANES, 2 * width), lambda i: (0, 0)),
            pl.BlockSpec((n_pad, HEAD_DIM), lambda i: (0, 0)),
            pl.BlockSpec((n_pad, HEAD_DIM), lambda i: (0, 0)),
            pl.BlockSpec((n_pad, HEAD_DIM), lambda i: (0, 0)),
        ],
        out_specs=pl.BlockSpec((1, n_pad, HEAD_DIM), lambda i: (i, 0, 0)),
        out_shape=jax.ShapeDtypeStruct((n4, n_pad, HEAD_DIM), bf16),
        compiler_params=_cparams("parallel"),
        name="compress",
    )(t2, w1, w2, cp, ccosf, csina, csinb)


def _cmp_topk_kernel(q_ref, kc_ref, vc_ref, m_ref, o_ref, selb_ref, key_ref, *, n_slc):
    i = pl.program_id(1)
    t0 = i * Q_BLK
    rows = GROUP * Q_BLK
    kc = kc_ref[0]
    vc = vc_ref[0]
    n_pad = kc.shape[0]
    q4 = jnp.concatenate([q_ref[:, r * HEAD_DIM:(r + 1) * HEAD_DIM] for r in range(GROUP)], axis=0)
    s = _dot_nt(q4, kc)
    t = t0 + (lax.broadcasted_iota(jnp.int32, (rows, n_pad), 0) & (Q_BLK - 1))
    cmp_end = lax.broadcasted_iota(jnp.int32, (rows, n_pad), 1) * CMP_STRIDE + (CMP_BLOCK - 1)
    mask = cmp_end <= t
    s = jnp.where(mask, s, NEG_INF)
    m = jnp.max(s, axis=-1, keepdims=True)
    p = jnp.where(mask, jnp.exp(s - m), 0.0)
    l = jnp.sum(p, axis=-1, keepdims=True)
    p = p / jnp.where(l > 0.0, l, 1.0)
    o = _dot(p.astype(bf16), vc)
    for r in range(GROUP):
        o_ref[:, r * HEAD_DIM:(r + 1) * HEAD_DIM] = o[r * Q_BLK:(r + 1) * Q_BLK, :]

    p_imp = p[0:Q_BLK]
    for r in range(1, GROUP):
        p_imp = p_imp + p[r * Q_BLK:(r + 1) * Q_BLK]
    p_slc_t = _split_dot(p_imp, m_ref[...]).T

    blk = lax.broadcasted_iota(jnp.int32, (LANES, Q_BLK), 0)
    blk_t = (t0 + lax.broadcasted_iota(jnp.int32, (LANES, Q_BLK), 1)) >> SLC_SHIFT
    causal = blk <= blk_t
    forced = (blk == 0) | (blk == blk_t) | (blk == blk_t - 1)
    imp = jnp.where(causal, jnp.where(forced, FORCE_SCORE, p_slc_t), NEG_SCORE)
    key_ref[...] = imp

    def body(b2, rank):
        for u in range(2):
            b = b2 * 2 + u
            row = key_ref[pl.ds(b, 1), :]
            ahead = (row > imp) | ((row == imp) & (b < blk))
            rank = rank + jnp.where(ahead, 1.0, 0.0)
        return rank

    n_live = jnp.minimum((t0 + Q_BLK - 1) // SLC_BLOCK + 1, n_slc)
    rank = lax.fori_loop(0, (n_live + 1) // 2, body, jnp.zeros((LANES, Q_BLK), f32))
    sel = (rank < float(min(SLC_TOP_N, n_slc))) & causal
    selb_ref[0] = jnp.where(sel, 0.0, SEL_MASK_BIAS).T.astype(bf16)


def _cmp_topk(P, kvc, m_mat, n_slc):
    S = P.shape[0]
    n_pad = kvc.shape[1]
    gw = GROUP * HEAD_DIM
    return pl.pallas_call(
        functools.partial(_cmp_topk_kernel, n_slc=n_slc),
        grid=(N_KV_NSA, S // Q_BLK),
        in_specs=[
            pl.BlockSpec((Q_BLK, gw), lambda g, i: (i, g)),
            pl.BlockSpec((1, n_pad, HEAD_DIM), lambda g, i: (g, 0, 0)),
            pl.BlockSpec((1, n_pad, HEAD_DIM), lambda g, i: (N_KV_NSA + g, 0, 0)),
            pl.BlockSpec((n_pad, LANES), lambda g, i: (0, 0)),
        ],
        out_specs=[
            pl.BlockSpec((Q_BLK, gw), lambda g, i: (i, g)),
            pl.BlockSpec((1, Q_BLK, LANES), lambda g, i: (g, i, 0)),
        ],
        out_shape=[
            jax.ShapeDtypeStruct((S, N_HEADS_NSA * HEAD_DIM), f32),
            jax.ShapeDtypeStruct((N_KV_NSA, S, LANES), bf16),
        ],
        scratch_shapes=[pltpu.VMEM((LANES, Q_BLK), f32)],
        compiler_params=_cparams("parallel", "parallel"),
        name="cmp_topk",
    )(P, kvc, kvc, m_mat)


def _nsa_kernel(q_ref, ks_ref, vs_ref, kw_ref, vw_ref, e_ref, selb_ref, ocmp_ref, gate_ref,
                gg_ref, o_ref, *, q_blk, kv_blk, win_span):
    i = pl.program_id(1)
    t0 = i * q_blk
    rows = GROUP * q_blk
    S = ks_ref.shape[0]
    q4 = jnp.concatenate([q_ref[:, r * HEAD_DIM:(r + 1) * HEAD_DIM] for r in range(GROUP)], axis=0)
    selb = selb_ref[0]
    q_aug = jnp.concatenate([q4, jnp.concatenate([selb] * GROUP, axis=0)], axis=1)
    half = kv_blk // 2
    ones = jnp.ones((half, HEAD_DIM), bf16)

    def tile(kb, carry, causal_mask):
        m, acc = carry
        s = []
        for hf in range(2):
            k0 = pl.multiple_of(kb * kv_blk + hf * half, half)
            k_aug = jnp.concatenate([ks_ref[pl.ds(k0, half), :], e_ref[pl.ds(k0, half), :]], axis=1)
            s_h = _dot_nt(q_aug, k_aug)
            if causal_mask:
                t = t0 + (lax.broadcasted_iota(jnp.int32, (rows, half), 0) & (q_blk - 1))
                s_h = jnp.where(k0 + lax.broadcasted_iota(jnp.int32, (rows, half), 1) <= t, s_h, NEG_INF)
            s.append(s_h)
        for hf in range(2):
            k0 = pl.multiple_of(kb * kv_blk + hf * half, half)
            m_new = jnp.maximum(m, jnp.max(s[hf], axis=-1, keepdims=True))
            p = jnp.exp(s[hf] - m_new).astype(bf16)
            v_aug = jnp.concatenate([vs_ref[pl.ds(k0, half), :], ones], axis=1)
            acc = jnp.exp(m - m_new) * acc + _dot(p, v_aug)
            m = m_new
        return m, acc

    kd = t0 // kv_blk
    init = (jnp.full((rows, 1), NEG_INF, f32), jnp.zeros((rows, 2 * HEAD_DIM), f32))
    carry = lax.fori_loop(0, kd, lambda kb, c: tile(kb, c, False), init)
    _, acc = tile(kd, carry, True)
    o_slc = acc[:, :HEAD_DIM] / acc[:, HEAD_DIM:]

    w0 = pl.multiple_of(jnp.clip(t0 - WINDOW, 0, S - win_span), q_blk)
    s = _dot_nt(q4, kw_ref[pl.ds(w0, win_span), :])
    t = t0 + (lax.broadcasted_iota(jnp.int32, (rows, win_span), 0) & (q_blk - 1))
    diff = t - (w0 + lax.broadcasted_iota(jnp.int32, (rows, win_span), 1))
    s = jnp.where((diff >= 0) & (diff < WINDOW), s, NEG_INF)
    p = jnp.exp(s - jnp.max(s, axis=-1, keepdims=True))
    l_w = jnp.sum(p, axis=-1, keepdims=True)
    o_win = _dot(p.astype(bf16), vw_ref[pl.ds(w0, win_span), :]) / l_w

    gates = gate_ref[...]
    g = pl.program_id(0)
    for r in range(GROUP):
        rs = slice(r * q_blk, (r + 1) * q_blk)
        cs = slice(r * HEAD_DIM, (r + 1) * HEAD_DIM)
        h_lane = lax.broadcasted_iota(jnp.int32, gates.shape, 1) - (g * GROUP + r)

        def gate_col(branch):
            return jnp.sum(jnp.where(h_lane == branch * N_HEADS_NSA, gates, 0.0), axis=-1, keepdims=True)

        o = gate_col(0) * ocmp_ref[:, cs] + gate_col(1) * o_slc[rs] + gate_col(2) * o_win[rs]
        o_ref[:, cs] = (_rms(o) * gg_ref[:, cs]).astype(bf16)


def _nsa(P, e_mat, selb, o_cmp, gates, gg, q_blk, kv_blk, win_span):
    S = P.shape[0]
    gw = GROUP * HEAD_DIM
    full = lambda c: pl.BlockSpec((S, HEAD_DIM), lambda g, i, c=c: (0, c + g))
    return pl.pallas_call(
        functools.partial(_nsa_kernel, q_blk=q_blk, kv_blk=kv_blk, win_span=win_span),
        grid=(N_KV_NSA, S // q_blk),
        in_specs=[
            pl.BlockSpec((q_blk, gw), lambda g, i: (i, g)),
            full(COL_KS), full(COL_VS), full(COL_KW), full(COL_VW),
            pl.BlockSpec((S, LANES), lambda g, i: (0, 0)),
            pl.BlockSpec((1, q_blk, LANES), lambda g, i: (g, i, 0)),
            pl.BlockSpec((q_blk, gw), lambda g, i: (i, g)),
            pl.BlockSpec((q_blk, LANES), lambda g, i: (i, 0)),
            pl.BlockSpec((1, gw), lambda g, i: (0, g)),
        ],
        out_specs=pl.BlockSpec((q_blk, gw), lambda g, i: (i, g)),
        out_shape=jax.ShapeDtypeStruct((S, N_HEADS_NSA * HEAD_DIM), bf16),
        compiler_params=_cparams("parallel", "parallel"),
        name="nsa",
    )(P, P, P, P, P, e_mat, selb, o_cmp, gates, gg)


def _sb_span(q, k, v, u, t0, k0, carry, limit=None):
    n = k.shape[0] // SB_KV_BLK
    z = _dot_nt(q, k)
    t = t0 + lax.broadcasted_iota(jnp.int32, z.shape, 0)
    if limit is not None:
        t = jnp.minimum(t, limit)
    valid = (k0 + lax.broadcasted_iota(jnp.int32, z.shape, 1)) < t
    log_1m = jnp.where(valid, -(jnp.maximum(z, 0.0) + jnp.log1p(jnp.exp(-jnp.abs(z)))), 0.0)
    after = [None] * n
    for c in reversed(range(n)):
        part = log_1m[:, c * SB_KV_BLK:(c + 1) * SB_KV_BLK]
        after[c] = _split_dot(part, u) + carry
        carry = carry + jnp.sum(part, axis=-1, keepdims=True)
    after = jnp.concatenate(after, axis=1) if n > 1 else after[0]
    a = jnp.where(valid, jnp.exp(z + log_1m + after), 0.0)
    return _dot(a.astype(bf16), v), carry


def _sb_kernel(q_ref, k_ref, v_ref, u_ref, gg_ref, o_ref):
    i = pl.program_id(1)
    t0 = i * Q_BLK
    u = u_ref[...]
    span = SB_SPAN * SB_KV_BLK
    heads = [slice(h * HEAD_DIM, (h + 1) * HEAD_DIM) for h in range(SB_HEADS)]
    qs = [q_ref[:, hs] for hs in heads]

    s0 = pl.multiple_of(jnp.maximum(t0 - (SB_SPAN - 1) * SB_KV_BLK, 0), SB_KV_BLK)
    zero = jnp.zeros((Q_BLK, 1), f32)
    accs, carries = [], []
    for h, hs in enumerate(heads):
        acc, carry = _sb_span(qs[h], k_ref[pl.ds(s0, span), hs], v_ref[pl.ds(s0, span), hs], u, t0, s0, zero)
        accs.append(acc)
        carries.append(carry)

    def carry_max(cs):
        return jnp.max(functools.reduce(jnp.maximum, cs))

    def cond(c):
        nt, cmax, _, _ = c
        return (nt >= 0) & (cmax > SB_UNDERFLOW)

    def body(c):
        nt, _, cs, acs = c
        k0 = pl.multiple_of(jnp.maximum(nt - (SB_LOOP_SPAN - 1), 0) * SB_KV_BLK, SB_KV_BLK)
        limit = (nt + 1) * SB_KV_BLK
        width = SB_LOOP_SPAN * SB_KV_BLK
        new_cs, new_acs = [], []
        for h, hs in enumerate(heads):
            acc, carry = _sb_span(qs[h], k_ref[pl.ds(k0, width), hs], v_ref[pl.ds(k0, width), hs],
                                  u, t0, k0, cs[h], limit)
            new_cs.append(carry)
            new_acs.append(acs[h] + acc)
        return nt - SB_LOOP_SPAN, carry_max(new_cs), tuple(new_cs), tuple(new_acs)

    init = (s0 // SB_KV_BLK - 1, carry_max(carries), tuple(carries), tuple(accs))
    _, _, _, accs = lax.while_loop(cond, body, init)
    for h, hs in enumerate(heads):
        o_ref[:, hs] = (_rms(accs[h]) * gg_ref[:, hs]).astype(bf16)


def _sb(P, u_mat, gg):
    S = P.shape[0]
    hw = SB_HEADS * HEAD_DIM
    return pl.pallas_call(
        _sb_kernel,
        grid=(N_HEADS_SB // SB_HEADS, S // Q_BLK),
        in_specs=[
            pl.BlockSpec((Q_BLK, hw), lambda h, i: (i, COL_SBQ // SB_HEADS + h)),
            pl.BlockSpec((S, hw), lambda h, i: (0, COL_SBK // SB_HEADS + h)),
            pl.BlockSpec((S, hw), lambda h, i: (0, COL_SBV // SB_HEADS + h)),
            pl.BlockSpec((SB_KV_BLK, SB_KV_BLK), lambda h, i: (0, 0)),
            pl.BlockSpec((1, hw), lambda h, i: (0, N_HEADS_NSA // SB_HEADS + h)),
        ],
        out_specs=pl.BlockSpec((Q_BLK, hw), lambda h, i: (i, h)),
        out_shape=jax.ShapeDtypeStruct((S, N_HEADS_SB * HEAD_DIM), bf16),
        compiler_params=_cparams("parallel", "parallel"),
        name="sb",
    )(P, P, P, u_mat, gg)


def _out_proj_kernel(h_ref, a_ref, b_ref, wa_ref, wb_ref, o_ref):
    o_ref[...] = h_ref[...] + _dot(a_ref[...], wa_ref[...]) + _dot(b_ref[...], wb_ref[...])


def _out_proj(h, o_nsa, o_sb, w_out, tm, tn):
    S, D = h.shape
    ka = o_nsa.shape[1]
    kb = o_sb.shape[1]
    return pl.pallas_call(
        _out_proj_kernel,
        grid=(S // tm, D // tn),
        in_specs=[
            pl.BlockSpec((tm, tn), lambda i, j: (i, j)),
            pl.BlockSpec((tm, ka), lambda i, j: (i, 0)),
            pl.BlockSpec((tm, kb), lambda i, j: (i, 0)),
            pl.BlockSpec((ka, tn), lambda i, j: (0, j)),
            pl.BlockSpec((kb, tn), lambda i, j: (ka // kb, j)),
        ],
        out_specs=pl.BlockSpec((tm, tn), lambda i, j: (i, j)),
        out_shape=jax.ShapeDtypeStruct((S, D), f32),
        compiler_params=_cparams("parallel", "parallel"),
        name="out_proj",
    )(h, o_nsa, o_sb, w_out, w_out)


def _ffn_kernel(xn_ref, h_ref, wg_ref, wu_ref, wd_ref, gf_ref, o_ref, *, final_norm):
    f = pl.program_id(1)

    @pl.when(f == 0)
    def _():
        o_ref[...] = h_ref[...]

    xn = xn_ref[...]
    gate = _dot(xn, wg_ref[...])
    up = _dot(xn, wu_ref[...])
    act = (gate * jax.nn.sigmoid(gate) * up).astype(bf16)
    o_ref[...] += _dot(act, wd_ref[...])

    if final_norm:
        @pl.when(f == pl.num_programs(1) - 1)
        def _():
            o_ref[...] = _rms(o_ref[...]) * gf_ref[...]


def _ffn(xn, h, wg, wu, wd, gf, tm, tf, final_norm):
    S, D = h.shape
    F = wg.shape[1]
    return pl.pallas_call(
        functools.partial(_ffn_kernel, final_norm=final_norm),
        grid=(S // tm, F // tf),
        in_specs=[
            pl.BlockSpec((tm, D), lambda i, f: (i, 0)),
            pl.BlockSpec((tm, D), lambda i, f: (i, 0)),
            pl.BlockSpec((D, tf), lambda i, f: (0, f)),
            pl.BlockSpec((D, tf), lambda i, f: (0, f)),
            pl.BlockSpec((tf, D), lambda i, f: (f, 0)),
            pl.BlockSpec((1, D), lambda i, f: (0, 0)),
        ],
        out_specs=pl.BlockSpec((tm, D), lambda i, f: (i, 0)),
        out_shape=jax.ShapeDtypeStruct((S, D), f32),
        compiler_params=_cparams("parallel", "arbitrary"),
        name="ffn",
    )(xn, h, wg, wu, wd, gf)


def _rope_tables(pos):
    inv = ROPE_THETA ** (-jnp.arange(0, ROPE_DIM, 2, dtype=f32) / ROPE_DIM)
    ang = pos.astype(f32)[:, None] * inv
    cos, sin = jnp.cos(ang), jnp.sin(ang)
    n = pos.shape[0]
    rest = HEAD_DIM - ROPE_DIM
    cosf = jnp.concatenate([cos, cos, jnp.ones((n, rest), f32)], axis=1)
    sina = jnp.concatenate([-sin, jnp.zeros((n, HEAD_DIM - ROPE_HALF), f32)], axis=1)
    sinb = jnp.concatenate([jnp.zeros((n, ROPE_HALF), f32), sin, jnp.zeros((n, rest), f32)], axis=1)
    return cosf, sina, sinb


def _importance_matrix(n_pad):
    ratio = SLC_BLOCK // CMP_STRIDE
    nb = CMP_BLOCK // CMP_STRIDE
    m = np.zeros((n_pad, LANES), np.float32)
    for n in range(n_pad):
        for k in range(nb):
            j = (n + k) // ratio
            if j < LANES:
                m[n, j] += 1.0
    return m


def kernel(x, positions, norm_mix, w_in, b_gate, cmp_pos, w_cmp_k1, w_cmp_k2, w_cmp_v1, w_cmp_v2,
           norm_grp, w_out, norm_ffn, w_gate, w_up, w_down, norm_final):
    B, S, D = x.shape
    assert B == 1 and S % KV_BLK == 0 and S // SLC_BLOCK <= LANES
    depth = w_in.shape[0]
    n_slc = S // SLC_BLOCK
    n_pad = S // CMP_STRIDE
    win_span = WINDOW + NSA_Q_BLK
    tm = 512
    tm_proj = 1024 if S % 1024 == 0 else tm
    pos = positions[0]

    cosf, sina, sinb = _rope_tables(pos)
    cmp_end = jnp.minimum(jnp.arange(n_pad) * CMP_STRIDE + CMP_BLOCK - 1, S - 1)
    ccosf, csina, csinb = _rope_tables(pos[cmp_end])

    m_mat = jnp.asarray(_importance_matrix(n_pad), bf16)
    e_np = (np.arange(S)[:, None] // SLC_BLOCK == np.arange(LANES)[None, :])
    e_mat = jnp.asarray(e_np, bf16)
    u_np = np.arange(SB_KV_BLK)[:, None] > np.arange(SB_KV_BLK)[None, :]
    u_mat = jnp.asarray(u_np, bf16)

    q_end = N_HEADS_NSA * HEAD_DIM + 6 * N_KV_NSA * HEAD_DIM
    n_gate = N_BRANCH * N_HEADS_NSA
    w_main = jnp.concatenate([w_in[:, :, :q_end], w_in[:, :, q_end + n_gate:]], axis=2).astype(bf16)
    w_g = jnp.pad(w_in[:, :, q_end:q_end + n_gate], ((0, 0), (0, 0), (0, LANES - n_gate))).astype(bf16)
    b_g = jnp.pad(b_gate, ((0, 0), (0, LANES - n_gate)))[:, None, :]
    w1 = jnp.stack([w_cmp_k1, w_cmp_v1], axis=1).astype(bf16)
    w2 = jnp.stack([w_cmp_k2, w_cmp_v2], axis=1).astype(bf16)
    cp = jnp.broadcast_to(cmp_pos.reshape(depth, 1, CMP_BLOCK * HEAD_DIM),
                          (depth, SUBLANES, CMP_BLOCK * HEAD_DIM)).astype(bf16)
    w_out_b = w_out.astype(bf16)
    w_gate_b, w_up_b, w_down_b = w_gate.astype(bf16), w_up.astype(bf16), w_down.astype(bf16)

    h = x[0]
    for layer in range(depth):
        xn, gates = _norm_gate(h, norm_mix[layer][None, :], w_g[layer], b_g[layer], tm)
        P = _proj(xn, w_main[layer], cosf, sina, sinb, tm_proj)
        n_c = 2 * N_KV_NSA
        t2 = P[:, COL_KC * HEAD_DIM:(COL_KC + n_c) * HEAD_DIM].reshape(S, n_c, HEAD_DIM)
        t2 = t2.transpose(1, 0, 2).reshape(n_c, n_pad, CMP_STRIDE * HEAD_DIM)
        kvc = _compress(t2, w1[layer], w2[layer], cp[layer], ccosf, csina, csinb)
        o_cmp, selb = _cmp_topk(P, kvc, m_mat, n_slc)
        gg = norm_grp[layer][None, :]
        o_nsa = _nsa(P, e_mat, selb, o_cmp, gates, gg, NSA_Q_BLK, KV_BLK, win_span)
        o_sb = _sb(P, u_mat, gg)
        h = _out_proj(h, o_nsa, o_sb, w_out_b[layer], tm, 512)
        xn = _norm(h, norm_ffn[layer][None, :], tm)
        h = _ffn(xn, h, w_gate_b[layer], w_up_b[layer], w_down_b[layer],
                 norm_final[None, :], tm, 512, layer == depth - 1)
    return h[None]
```

```python
import functools

import jax
import jax.numpy as jnp
import numpy as np
from jax import lax
from jax.experimental import pallas as pl
from jax.experimental.pallas import tpu as pltpu

HEAD_DIM = 128
N_HEADS_NSA = 8
N_KV_NSA = 2
GROUP = N_HEADS_NSA // N_KV_NSA
N_HEADS_SB = 8
ROPE_DIM = HEAD_DIM // 4
ROPE_HALF = ROPE_DIM // 2
ROPE_THETA = 500000.0
CMP_BLOCK = 32
CMP_STRIDE = 16
CMP_HIDDEN = 256
SLC_BLOCK = 64
SLC_SHIFT = SLC_BLOCK.bit_length() - 1
SLC_TOP_N = 16
WINDOW = 512
N_BRANCH = 3
EPS = 1e-6
NEG_INF = -1e30
FORCE_SCORE = 1e4
NEG_SCORE = -1e4
SCALE = HEAD_DIM ** -0.5

LANES = 128
SUBLANES = 8
SEL_MASK_BIAS = -30000.0
SB_UNDERFLOW = -105.0
VMEM_LIMIT = 52 * 1024 * 1024

Q_BLK = 128
NSA_Q_BLK = 256
KV_BLK = 512
SB_KV_BLK = 128
SB_SPAN = 5
SB_LOOP_SPAN = 1
SB_HEADS = 4

COL_Q = 0
COL_KC = 8
COL_KS, COL_VS, COL_KW, COL_VW = 12, 14, 16, 18
COL_SBQ, COL_SBK, COL_SBV = 20, 28, 36
N_MAIN = 44 * HEAD_DIM

f32 = jnp.float32
bf16 = jnp.bfloat16


def _cparams(*sem):
    return pltpu.CompilerParams(dimension_semantics=sem, vmem_limit_bytes=VMEM_LIMIT)


def _dot(a, b):
    return jnp.dot(a, b, preferred_element_type=f32)


def _dot_nt(a, b):
    return lax.dot_general(a, b, (((1,), (1,)), ((), ())), preferred_element_type=f32)


def _split_dot(a, b):
    hi = a.astype(bf16)
    lo = (a - hi.astype(f32)).astype(bf16)
    return _dot(hi, b) + _dot(lo, b)


def _rope(x, cosf, sina, sinb):
    reps = x.shape[-1] // HEAD_DIM
    w = x.shape[-1]
    cosf, sina, sinb = (jnp.concatenate([t] * reps, axis=1) if reps > 1 else t for t in (cosf, sina, sinb))
    return (x * cosf + pltpu.roll(x, w - ROPE_HALF, axis=1) * sina
            + pltpu.roll(x, ROPE_HALF, axis=1) * sinb)


def _rms(x):
    return x * lax.rsqrt(jnp.mean(x * x, axis=-1, keepdims=True) + EPS)


def _norm_kernel(h_ref, g_ref, o_ref):
    o_ref[...] = (_rms(h_ref[...]) * g_ref[...]).astype(bf16)


def _norm_gate_kernel(h_ref, g_ref, wg_ref, bg_ref, o_ref, gate_ref):
    xn = (_rms(h_ref[...]) * g_ref[...]).astype(bf16)
    o_ref[...] = xn
    gate_ref[...] = jax.nn.sigmoid(_dot(xn, wg_ref[...]) + bg_ref[...])


def _norm(h, g, tm):
    S, D = h.shape
    return pl.pallas_call(
        _norm_kernel,
        grid=(S // tm,),
        in_specs=[pl.BlockSpec((tm, D), lambda i: (i, 0)), pl.BlockSpec((1, D), lambda i: (0, 0))],
        out_specs=pl.BlockSpec((tm, D), lambda i: (i, 0)),
        out_shape=jax.ShapeDtypeStruct((S, D), bf16),
        compiler_params=_cparams("parallel"),
        name="norm",
    )(h, g)


def _norm_gate(h, g, wg, bg, tm):
    S, D = h.shape
    return pl.pallas_call(
        _norm_gate_kernel,
        grid=(S // tm,),
        in_specs=[
            pl.BlockSpec((tm, D), lambda i: (i, 0)),
            pl.BlockSpec((1, D), lambda i: (0, 0)),
            pl.BlockSpec((D, LANES), lambda i: (0, 0)),
            pl.BlockSpec((1, LANES), lambda i: (0, 0)),
        ],
        out_specs=[pl.BlockSpec((tm, D), lambda i: (i, 0)), pl.BlockSpec((tm, LANES), lambda i: (i, 0))],
        out_shape=[jax.ShapeDtypeStruct((S, D), bf16), jax.ShapeDtypeStruct((S, LANES), f32)],
        compiler_params=_cparams("parallel"),
        name="norm_gate",
    )(h, g, wg, bg)


PROJ_TN = 512
PROJ_HEADS = PROJ_TN // HEAD_DIM


def _proj_kernel(x_ref, w_ref, cosf_ref, sina_ref, sinb_ref, o_ref):
    j = pl.program_id(1)
    acc = _dot(x_ref[...], w_ref[...])
    tables = (cosf_ref[...], sina_ref[...], sinb_ref[...])
    q_tiles = N_HEADS_NSA // PROJ_HEADS
    k_width = N_KV_NSA * HEAD_DIM
    is_q = j < q_tiles
    is_kv = (j == COL_KS // PROJ_HEADS) | (j == COL_KW // PROJ_HEADS)
    is_sbq = (j >= COL_SBQ // PROJ_HEADS) & (j < COL_SBK // PROJ_HEADS)

    @pl.when(is_q)
    def _():
        o_ref[...] = (_rope(acc, *tables) * SCALE).astype(bf16)

    @pl.when(is_kv)
    def _():
        o_ref[:, :k_width] = _rope(acc[:, :k_width], *tables).astype(bf16)
        o_ref[:, k_width:] = acc[:, k_width:].astype(bf16)

    @pl.when(is_sbq)
    def _():
        o_ref[...] = (acc * SCALE).astype(bf16)

    @pl.when(jnp.logical_not(is_q | is_kv | is_sbq))
    def _():
        o_ref[...] = acc.astype(bf16)


def _proj(xn, w, cosf, sina, sinb, tm):
    S, D = xn.shape
    N = w.shape[1]
    tn = PROJ_TN
    assert COL_VS == COL_KS + N_KV_NSA and COL_KS % PROJ_HEADS == 0 and COL_KW % PROJ_HEADS == 0
    table = pl.BlockSpec((tm, HEAD_DIM), lambda i, j: (i, 0))
    return pl.pallas_call(
        _proj_kernel,
        grid=(S // tm, N // tn),
        in_specs=[
            pl.BlockSpec((tm, D), lambda i, j: (i, 0)),
            pl.BlockSpec((D, tn), lambda i, j: (0, j)),
            table, table, table,
        ],
        out_specs=pl.BlockSpec((tm, tn), lambda i, j: (i, j)),
        out_shape=jax.ShapeDtypeStruct((S, N), bf16),
        compiler_params=_cparams("parallel", "parallel"),
        name="proj",
    )(xn, w, cosf, sina, sinb)


def _gelu_tanh(x):
    return 0.5 * x * (1.0 + jnp.tanh(np.sqrt(2.0 / np.pi) * (x + 0.044715 * (x * x * x))))


def _compress_kernel(t_ref, w1_ref, w2_ref, cp_ref, cosf_ref, sina_ref, sinb_ref, o_ref):
    i = pl.program_id(0)
    t = t_ref[0]
    half = CMP_STRIDE * HEAD_DIM
    a = _dot(t, w1_ref[0, :half, :])
    b = _dot(t, w1_ref[0, half:, :])
    n_pad = a.shape[0]
    bias = _dot(cp_ref[...], w1_ref[0])[0:1, :]
    hid = a + pltpu.roll(b, n_pad - 1, axis=0) + bias
    c = _dot(_gelu_tanh(hid).astype(bf16), w2_ref[0])

    @pl.when(i < N_KV_NSA)
    def _():
        o_ref[0] = _rope(c, cosf_ref[...], sina_ref[...], sinb_ref[...]).astype(bf16)

    @pl.when(i >= N_KV_NSA)
    def _():
        o_ref[0] = c.astype(bf16)


def _compress(t2, w1, w2, cp, ccosf, csina, csinb):
    n4, n_pad, width = t2.shape
    return pl.pallas_call(
        _compress_kernel,
        grid=(n4,),
        in_specs=[
            pl.BlockSpec((1, n_pad, width), lambda i: (i, 0, 0)),
            pl.BlockSpec((1, 2 * width, CMP_HIDDEN), lambda i: (i // N_KV_NSA, 0, 0)),
            pl.BlockSpec((1, CMP_HIDDEN, HEAD_DIM), lambda i: (i // N_KV_NSA, 0, 0)),
            pl.BlockSpec((SUBLANES, 2 * width), lambda i: (0, 0)),
            pl.BlockSpec((n_pad, HEAD_DIM), lambda i: (0, 0)),
            pl.BlockSpec((n_pad, HEAD_DIM), lambda i: (0, 0)),
            pl.BlockSpec((n_pad, HEAD_DIM), lambda i: (0, 0)),
        ],
        out_specs=pl.BlockSpec((1, n_pad, HEAD_DIM), lambda i: (i, 0, 0)),
        out_shape=jax.ShapeDtypeStruct((n4, n_pad, HEAD_DIM), bf16),
        compiler_params=_cparams("parallel"),
        name="compress",
    )(t2, w1, w2, cp, ccosf, csina, csinb)


def _cmp_topk_kernel(q_ref, kc_ref, vc_ref, m_ref, o_ref, selb_ref, key_ref, *, n_slc):
    i = pl.program_id(1)
    t0 = i * Q_BLK
    rows = GROUP * Q_BLK
    kc = kc_ref[0]
    vc = vc_ref[0]
    n_pad = kc.shape[0]
    q4 = jnp.concatenate([q_ref[:, r * HEAD_DIM:(r + 1) * HEAD_DIM] for r in range(GROUP)], axis=0)
    s = _dot_nt(q4, kc)
    t = t0 + (lax.broadcasted_iota(jnp.int32, (rows, n_pad), 0) & (Q_BLK - 1))
    cmp_end = lax.broadcasted_iota(jnp.int32, (rows, n_pad), 1) * CMP_STRIDE + (CMP_BLOCK - 1)
    mask = cmp_end <= t
    s = jnp.where(mask, s, NEG_INF)
    m = jnp.max(s, axis=-1, keepdims=True)
    p = jnp.where(mask, jnp.exp(s - m), 0.0)
    l = jnp.sum(p, axis=-1, keepdims=True)
    p = p / jnp.where(l > 0.0, l, 1.0)
    o = _dot(p.astype(bf16), vc)
    for r in range(GROUP):
        o_ref[:, r * HEAD_DIM:(r + 1) * HEAD_DIM] = o[r * Q_BLK:(r + 1) * Q_BLK, :]

    p_imp = p[0:Q_BLK]
    for r in range(1, GROUP):
        p_imp = p_imp + p[r * Q_BLK:(r + 1) * Q_BLK]
    p_slc_t = _split_dot(p_imp, m_ref[...]).T

    blk = lax.broadcasted_iota(jnp.int32, (LANES, Q_BLK), 0)
    blk_t = (t0 + lax.broadcasted_iota(jnp.int32, (LANES, Q_BLK), 1)) >> SLC_SHIFT
    causal = blk <= blk_t
    forced = (blk == 0) | (blk == blk_t) | (blk == blk_t - 1)
    imp = jnp.where(causal, jnp.where(forced, FORCE_SCORE, p_slc_t), NEG_SCORE)
    key_ref[...] = imp

    def body(b2, rank):
        for u in range(2):
            b = b2 * 2 + u
            row = key_ref[pl.ds(b, 1), :]
            ahead = (row > imp) | ((row == imp) & (b < blk))
            rank = rank + jnp.where(ahead, 1.0, 0.0)
        return rank

    n_live = jnp.minimum((t0 + Q_BLK - 1) // SLC_BLOCK + 1, n_slc)
    rank = lax.fori_loop(0, (n_live + 1) // 2, body, jnp.zeros((LANES, Q_BLK), f32))
    sel = (rank < float(min(SLC_TOP_N, n_slc))) & causal
    selb_ref[0] = jnp.where(sel, 0.0, SEL_MASK_BIAS).T.astype(bf16)


def _cmp_topk(P, kvc, m_mat, n_slc):
    S = P.shape[0]
    n_pad = kvc.shape[1]
    gw = GROUP * HEAD_DIM
    return pl.pallas_call(
        functools.partial(_cmp_topk_kernel, n_slc=n_slc),
        grid=(N_KV_NSA, S // Q_BLK),
        in_specs=[
            pl.BlockSpec((Q_BLK, gw), lambda g, i: (i, g)),
            pl.BlockSpec((1, n_pad, HEAD_DIM), lambda g, i: (g, 0, 0)),
            pl.BlockSpec((1, n_pad, HEAD_DIM), lambda g, i: (N_KV_NSA + g, 0, 0)),
            pl.BlockSpec((n_pad, LANES), lambda g, i: (0, 0)),
        ],
        out_specs=[
            pl.BlockSpec((Q_BLK, gw), lambda g, i: (i, g)),
            pl.BlockSpec((1, Q_BLK, LANES), lambda g, i: (g, i, 0)),
        ],
        out_shape=[
            jax.ShapeDtypeStruct((S, N_HEADS_NSA * HEAD_DIM), f32),
            jax.ShapeDtypeStruct((N_KV_NSA, S, LANES), bf16),
        ],
        scratch_shapes=[pltpu.VMEM((LANES, Q_BLK), f32)],
        compiler_params=_cparams("parallel", "parallel"),
        name="cmp_topk",
    )(P, kvc, kvc, m_mat)


def _nsa_kernel(q_ref, ks_ref, vs_ref, kw_ref, vw_ref, e_ref, selb_ref, ocmp_ref, gate_ref,
                gg_ref, o_ref, *, q_blk, kv_blk, win_span):
    i = pl.program_id(1)
    t0 = i * q_blk
    rows = GROUP * q_blk
    S = ks_ref.shape[0]
    q4 = jnp.concatenate([q_ref[:, r * HEAD_DIM:(r + 1) * HEAD_DIM] for r in range(GROUP)], axis=0)
    selb = selb_ref[0]
    q_aug = jnp.concatenate([q4, jnp.concatenate([selb] * GROUP, axis=0)], axis=1)
    half = kv_blk // 2
    ones = jnp.ones((half, HEAD_DIM), bf16)

    def tile(kb, carry, causal_mask):
        m, acc = carry
        s = []
        for hf in range(2):
            k0 = pl.multiple_of(kb * kv_blk + hf * half, half)
            k_aug = jnp.concatenate([ks_ref[pl.ds(k0, half), :], e_ref[pl.ds(k0, half), :]], axis=1)
            s_h = _dot_nt(q_aug, k_aug)
            if causal_mask:
                t = t0 + (lax.broadcasted_iota(jnp.int32, (rows, half), 0) & (q_blk - 1))
                s_h = jnp.where(k0 + lax.broadcasted_iota(jnp.int32, (rows, half), 1) <= t, s_h, NEG_INF)
            s.append(s_h)
        for hf in range(2):
            k0 = pl.multiple_of(kb * kv_blk + hf * half, half)
            m_new = jnp.maximum(m, jnp.max(s[hf], axis=-1, keepdims=True))
            p = jnp.exp(s[hf] - m_new).astype(bf16)
            v_aug = jnp.concatenate([vs_ref[pl.ds(k0, half), :], ones], axis=1)
            acc = jnp.exp(m - m_new) * acc + _dot(p, v_aug)
            m = m_new
        return m, acc

    kd = t0 // kv_blk
    init = (jnp.full((rows, 1), NEG_INF, f32), jnp.zeros((rows, 2 * HEAD_DIM), f32))
    carry = lax.fori_loop(0, kd, lambda kb, c: tile(kb, c, False), init)
    _, acc = tile(kd, carry, True)
    o_slc = acc[:, :HEAD_DIM] / acc[:, HEAD_DIM:]

    w0 = pl.multiple_of(jnp.clip(t0 - WINDOW, 0, S - win_span), q_blk)
    s = _dot_nt(q4, kw_ref[pl.ds(w0, win_span), :])
    t = t0 + (lax.broadcasted_iota(jnp.int32, (rows, win_span), 0) & (q_blk - 1))
    diff = t - (w0 + lax.broadcasted_iota(jnp.int32, (rows, win_span), 1))
    s = jnp.where((diff >= 0) & (diff < WINDOW), s, NEG_INF)
    p = jnp.exp(s - jnp.max(s, axis=-1, keepdims=True))
    l_w = jnp.sum(p, axis=-1, keepdims=True)
    o_win = _dot(p.astype(bf16), vw_ref[pl.ds(w0, win_span), :]) / l_w

    gates = gate_ref[...]
    g = pl.program_id(0)
    for r in range(GROUP):
        rs = slice(r * q_blk, (r + 1) * q_blk)
        cs = slice(r * HEAD_DIM, (r + 1) * HEAD_DIM)
        h_lane = lax.broadcasted_iota(jnp.int32, gates.shape, 1) - (g * GROUP + r)

        def gate_col(branch):
            return jnp.sum(jnp.where(h_lane == branch * N_HEADS_NSA, gates, 0.0), axis=-1, keepdims=True)

        o = gate_col(0) * ocmp_ref[:, cs] + gate_col(1) * o_slc[rs] + gate_col(2) * o_win[rs]
        o_ref[:, cs] = (_rms(o) * gg_ref[:, cs]).astype(bf16)


def _nsa(P, e_mat, selb, o_cmp, gates, gg, q_blk, kv_blk, win_span):
    S = P.shape[0]
    gw = GROUP * HEAD_DIM
    full = lambda c: pl.BlockSpec((S, HEAD_DIM), lambda g, i, c=c: (0, c + g))
    return pl.pallas_call(
        functools.partial(_nsa_kernel, q_blk=q_blk, kv_blk=kv_blk, win_span=win_span),
        grid=(N_KV_NSA, S // q_blk),
        in_specs=[
            pl.BlockSpec((q_blk, gw), lambda g, i: (i, g)),
            full(COL_KS), full(COL_VS), full(COL_KW), full(COL_VW),
            pl.BlockSpec((S, LANES), lambda g, i: (0, 0)),
            pl.BlockSpec((1, q_blk, LANES), lambda g, i: (g, i, 0)),
            pl.BlockSpec((q_blk, gw), lambda g, i: (i, g)),
            pl.BlockSpec((q_blk, LANES), lambda g, i: (i, 0)),
            pl.BlockSpec((1, gw), lambda g, i: (0, g)),
        ],
        out_specs=pl.BlockSpec((q_blk, gw), lambda g, i: (i, g)),
        out_shape=jax.ShapeDtypeStruct((S, N_HEADS_NSA * HEAD_DIM), bf16),
        compiler_params=_cparams("parallel", "parallel"),
        name="nsa",
    )(P, P, P, P, P, e_mat, selb, o_cmp, gates, gg)


def _sb_span(q, k, v, u, t0, k0, carry, limit=None):
    n = k.shape[0] // SB_KV_BLK
    z = _dot_nt(q, k)
    t = t0 + lax.broadcasted_iota(jnp.int32, z.shape, 0)
    if limit is not None:
        t = jnp.minimum(t, limit)
    valid = (k0 + lax.broadcasted_iota(jnp.int32, z.shape, 1)) < t
    log_1m = jnp.where(valid, -(jnp.maximum(z, 0.0) + jnp.log1p(jnp.exp(-jnp.abs(z)))), 0.0)
    after = [None] * n
    for c in reversed(range(n)):
        part = log_1m[:, c * SB_KV_BLK:(c + 1) * SB_KV_BLK]
        after[c] = _split_dot(part, u) + carry
        carry = carry + jnp.sum(part, axis=-1, keepdims=True)
    after = jnp.concatenate(after, axis=1) if n > 1 else after[0]
    a = jnp.where(valid, jnp.exp(z + log_1m + after), 0.0)
    return _dot(a.astype(bf16), v), carry


def _sb_kernel(q_ref, k_ref, v_ref, u_ref, gg_ref, o_ref):
    i = pl.program_id(1)
    t0 = i * Q_BLK
    u = u_ref[...]
    span = SB_SPAN * SB_KV_BLK
    heads = [slice(h * HEAD_DIM, (h + 1) * HEAD_DIM) for h in range(SB_HEADS)]
    qs = [q_ref[:, hs] for hs in heads]

    s0 = pl.multiple_of(jnp.maximum(t0 - (SB_SPAN - 1) * SB_KV_BLK, 0), SB_KV_BLK)
    zero = jnp.zeros((Q_BLK, 1), f32)
    accs, carries = [], []
    for h, hs in enumerate(heads):
        acc, carry = _sb_span(qs[h], k_ref[pl.ds(s0, span), hs], v_ref[pl.ds(s0, span), hs], u, t0, s0, zero)
        accs.append(acc)
        carries.append(carry)

    def carry_max(cs):
        return jnp.max(functools.reduce(jnp.maximum, cs))

    def cond(c):
        nt, cmax, _, _ = c
        return (nt >= 0) & (cmax > SB_UNDERFLOW)

    def body(c):
        nt, _, cs, acs = c
        k0 = pl.multiple_of(jnp.maximum(nt - (SB_LOOP_SPAN - 1), 0) * SB_KV_BLK, SB_KV_BLK)
        limit = (nt + 1) * SB_KV_BLK
        width = SB_LOOP_SPAN * SB_KV_BLK
        new_cs, new_acs = [], []
        for h, hs in enumerate(heads):
            acc, carry = _sb_span(qs[h], k_ref[pl.ds(k0, width), hs], v_ref[pl.ds(k0, width), hs],
                                  u, t0, k0, cs[h], limit)
            new_cs.append(carry)
            new_acs.append(acs[h] + acc)
        return nt - SB_LOOP_SPAN, carry_max(new_cs), tuple(new_cs), tuple(new_acs)

    init = (s0 // SB_KV_BLK - 1, carry_max(carries), tuple(carries), tuple(accs))
    _, _, _, accs = lax.while_loop(cond, body, init)
    for h, hs in enumerate(heads):
        o_ref[:, hs] = (_rms(accs[h]) * gg_ref[:, hs]).astype(bf16)


def _sb(P, u_mat, gg):
    S = P.shape[0]
    hw = SB_HEADS * HEAD_DIM
    return pl.pallas_call(
        _sb_kernel,
        grid=(N_HEADS_SB // SB_HEADS, S // Q_BLK),
        in_specs=[
            pl.BlockSpec((Q_BLK, hw), lambda h, i: (i, COL_SBQ // SB_HEADS + h)),
            pl.BlockSpec((S, hw), lambda h, i: (0, COL_SBK // SB_HEADS + h)),
            pl.BlockSpec((S, hw), lambda h, i: (0, COL_SBV // SB_HEADS + h)),
            pl.BlockSpec((SB_KV_BLK, SB_KV_BLK), lambda h, i: (0, 0)),
            pl.BlockSpec((1, hw), lambda h, i: (0, N_HEADS_NSA // SB_HEADS + h)),
        ],
        out_specs=pl.BlockSpec((Q_BLK, hw), lambda h, i: (i, h)),
        out_shape=jax.ShapeDtypeStruct((S, N_HEADS_SB * HEAD_DIM), bf16),
        compiler_params=_cparams("parallel", "parallel"),
        name="sb",
    )(P, P, P, u_mat, gg)


def _out_proj_kernel(h_ref, a_ref, b_ref, wa_ref, wb_ref, o_ref):
    o_ref[...] = h_ref[...] + _dot(a_ref[...], wa_ref[...]) + _dot(b_ref[...], wb_ref[...])


def _out_proj(h, o_nsa, o_sb, w_out, tm, tn):
    S, D = h.shape
    ka = o_nsa.shape[1]
    kb = o_sb.shape[1]
    return pl.pallas_call(
        _out_proj_kernel,
        grid=(S // tm, D // tn),
        in_specs=[
            pl.BlockSpec((tm, tn), lambda i, j: (i, j)),
            pl.BlockSpec((tm, ka), lambda i, j: (i, 0)),
            pl.BlockSpec((tm, kb), lambda i, j: (i, 0)),
            pl.BlockSpec((ka, tn), lambda i, j: (0, j)),
            pl.BlockSpec((kb, tn), lambda i, j: (ka // kb, j)),
        ],
        out_specs=pl.BlockSpec((tm, tn), lambda i, j: (i, j)),
        out_shape=jax.ShapeDtypeStruct((S, D), f32),
        compiler_params=_cparams("parallel", "parallel"),
        name="out_proj",
    )(h, o_nsa, o_sb, w_out, w_out)


def _ffn_kernel(xn_ref, h_ref, wg_ref, wu_ref, wd_ref, gf_ref, o_ref, *, final_norm):
    f = pl.program_id(1)

    @pl.when(f == 0)
    def _():
        o_ref[...] = h_ref[...]

    xn = xn_ref[...]
    gate = _dot(xn, wg_ref[...])
    up = _dot(xn, wu_ref[...])
    act = (gate * jax.nn.sigmoid(gate) * up).astype(bf16)
    o_ref[...] += _dot(act, wd_ref[...])

    if final_norm:
        @pl.when(f == pl.num_programs(1) - 1)
        def _():
            o_ref[...] = _rms(o_ref[...]) * gf_ref[...]


def _ffn(xn, h, wg, wu, wd, gf, tm, tf, final_norm):
    S, D = h.shape
    F = wg.shape[1]
    return pl.pallas_call(
        functools.partial(_ffn_kernel, final_norm=final_norm),
        grid=(S // tm, F // tf),
        in_specs=[
            pl.BlockSpec((tm, D), lambda i, f: (i, 0)),
            pl.BlockSpec((tm, D), lambda i, f: (i, 0)),
            pl.BlockSpec((D, tf), lambda i, f: (0, f)),
            pl.BlockSpec((D, tf), lambda i, f: (0, f)),
            pl.BlockSpec((tf, D), lambda i, f: (f, 0)),
            pl.BlockSpec((1, D), lambda i, f: (0, 0)),
        ],
        out_specs=pl.BlockSpec((tm, D), lambda i, f: (i, 0)),
        out_shape=jax.ShapeDtypeStruct((S, D), f32),
        compiler_params=_cparams("parallel", "arbitrary"),
        name="ffn",
    )(xn, h, wg, wu, wd, gf)


def _rope_tables(pos):
    inv = ROPE_THETA ** (-jnp.arange(0, ROPE_DIM, 2, dtype=f32) / ROPE_DIM)
    ang = pos.astype(f32)[:, None] * inv
    cos, sin = jnp.cos(ang), jnp.sin(ang)
    n = pos.shape[0]
    rest = HEAD_DIM - ROPE_DIM
    cosf = jnp.concatenate([cos, cos, jnp.ones((n, rest), f32)], axis=1)
    sina = jnp.concatenate([-sin, jnp.zeros((n, HEAD_DIM - ROPE_HALF), f32)], axis=1)
    sinb = jnp.concatenate([jnp.zeros((n, ROPE_HALF), f32), sin, jnp.zeros((n, rest), f32)], axis=1)
    return cosf, sina, sinb


def _importance_matrix(n_pad):
    ratio = SLC_BLOCK // CMP_STRIDE
    nb = CMP_BLOCK // CMP_STRIDE
    m = np.zeros((n_pad, LANES), np.float32)
    for n in range(n_pad):
        for k in range(nb):
            j = (n + k) // ratio
            if j < LANES:
                m[n, j] += 1.0
    return m


def kernel(x, positions, norm_mix, w_in, b_gate, cmp_pos, w_cmp_k1, w_cmp_k2, w_cmp_v1, w_cmp_v2,
           norm_grp, w_out, norm_ffn, w_gate, w_up, w_down, norm_final):
    B, S, D = x.shape
    assert B == 1 and S % KV_BLK == 0 and S // SLC_BLOCK <= LANES
    depth = w_in.shape[0]
    n_slc = S // SLC_BLOCK
    n_pad = S // CMP_STRIDE
    win_span = WINDOW + NSA_Q_BLK
    tm = 512
    tm_proj = 1024 if S % 1024 == 0 else tm
    pos = positions[0]

    cosf, sina, sinb = _rope_tables(pos)
    cmp_end = jnp.minimum(jnp.arange(n_pad) * CMP_STRIDE + CMP_BLOCK - 1, S - 1)
    ccosf, csina, csinb = _rope_tables(pos[cmp_end])

    m_mat = jnp.asarray(_importance_matrix(n_pad), bf16)
    e_np = (np.arange(S)[:, None] // SLC_BLOCK == np.arange(LANES)[None, :])
    e_mat = jnp.asarray(e_np, bf16)
    u_np = np.arange(SB_KV_BLK)[:, None] > np.arange(SB_KV_BLK)[None, :]
    u_mat = jnp.asarray(u_np, bf16)

    q_end = N_HEADS_NSA * HEAD_DIM + 6 * N_KV_NSA * HEAD_DIM
    n_gate = N_BRANCH * N_HEADS_NSA
    w_main = jnp.concatenate([w_in[:, :, :q_end], w_in[:, :, q_end + n_gate:]], axis=2).astype(bf16)
    w_g = jnp.pad(w_in[:, :, q_end:q_end + n_gate], ((0, 0), (0, 0), (0, LANES - n_gate))).astype(bf16)
    b_g = jnp.pad(b_gate, ((0, 0), (0, LANES - n_gate)))[:, None, :]
    w1 = jnp.stack([w_cmp_k1, w_cmp_v1], axis=1).astype(bf16)
    w2 = jnp.stack([w_cmp_k2, w_cmp_v2], axis=1).astype(bf16)
    cp = jnp.broadcast_to(cmp_pos.reshape(depth, 1, CMP_BLOCK * HEAD_DIM),
                          (depth, SUBLANES, CMP_BLOCK * HEAD_DIM)).astype(bf16)
    w_out_b = w_out.astype(bf16)
    w_gate_b, w_up_b, w_down_b = w_gate.astype(bf16), w_up.astype(bf16), w_down.astype(bf16)

    h = x[0]
    for layer in range(depth):
        xn, gates = _norm_gate(h, norm_mix[layer][None, :], w_g[layer], b_g[layer], tm)
        P = _proj(xn, w_main[layer], cosf, sina, sinb, tm_proj)
        n_c = 2 * N_KV_NSA
        t2 = P[:, COL_KC * HEAD_DIM:(COL_KC + n_c) * HEAD_DIM].reshape(S, n_c, HEAD_DIM)
        t2 = t2.transpose(1, 0, 2).reshape(n_c, n_pad, CMP_STRIDE * HEAD_DIM)
        kvc = _compress(t2, w1[layer], w2[layer], cp[layer], ccosf, csina, csinb)
        o_cmp, selb = _cmp_topk(P, kvc, m_mat, n_slc)
        gg = norm_grp[layer][None, :]
        o_nsa = _nsa(P, e_mat, selb, o_cmp, gates, gg, NSA_Q_BLK, KV_BLK, win_span)
        o_sb = _sb(P, u_mat, gg)
        h = _out_proj(h, o_nsa, o_sb, w_out_b[layer], tm, 512)
        xn = _norm(h, norm_ffn[layer][None, :], tm)
        h = _ffn(xn, h, w_gate_b[layer], w_up_b[layer], w_down_b[layer],
                 norm_final[None, :], tm, 512, layer == depth - 1)
    return h[None]
```

```python
import functools

import jax
import jax.numpy as jnp
import numpy as np
from jax import lax
from jax.experimental import pallas as pl
from jax.experimental.pallas import tpu as pltpu

HEAD_DIM = 128
N_HEADS_NSA = 8
N_KV_NSA = 2
GROUP = N_HEADS_NSA // N_KV_NSA
N_HEADS_SB = 8
ROPE_DIM = HEAD_DIM // 4
ROPE_HALF = ROPE_DIM // 2
ROPE_THETA = 500000.0
CMP_BLOCK = 32
CMP_STRIDE = 16
CMP_HIDDEN = 256
SLC_BLOCK = 64
SLC_SHIFT = SLC_BLOCK.bit_length() - 1
SLC_TOP_N = 16
WINDOW = 512
N_BRANCH = 3
EPS = 1e-6
NEG_INF = -1e30
FORCE_SCORE = 1e4
NEG_SCORE = -1e4
SCALE = HEAD_DIM ** -0.5

LANES = 128
SUBLANES = 8
SEL_MASK_BIAS = -30000.0
SB_UNDERFLOW = -105.0
VMEM_LIMIT = 52 * 1024 * 1024

Q_BLK = 128
NSA_Q_BLK = 256
KV_BLK = 512
SB_KV_BLK = 128
SB_SPAN = 5
SB_LOOP_SPAN = 1
SB_HEADS = 4

COL_Q = 0
COL_KC = 8
COL_KS, COL_VS, COL_KW, COL_VW = 12, 14, 16, 18
COL_SBQ, COL_SBK, COL_SBV = 20, 28, 36
N_MAIN = 44 * HEAD_DIM

f32 = jnp.float32
bf16 = jnp.bfloat16


def _cparams(*sem):
    return pltpu.CompilerParams(dimension_semantics=sem, vmem_limit_bytes=VMEM_LIMIT)


def _dot(a, b):
    return jnp.dot(a, b, preferred_element_type=f32)


def _dot_nt(a, b):
    return lax.dot_general(a, b, (((1,), (1,)), ((), ())), preferred_element_type=f32)


def _split_dot(a, b):
    hi = a.astype(bf16)
    lo = (a - hi.astype(f32)).astype(bf16)
    return _dot(hi, b) + _dot(lo, b)


def _rope(x, cosf, sina, sinb):
    reps = x.shape[-1] // HEAD_DIM
    w = x.shape[-1]
    cosf, sina, sinb = (jnp.concatenate([t] * reps, axis=1) if reps > 1 else t for t in (cosf, sina, sinb))
    return (x * cosf + pltpu.roll(x, w - ROPE_HALF, axis=1) * sina
            + pltpu.roll(x, ROPE_HALF, axis=1) * sinb)


def _rms(x):
    return x * lax.rsqrt(jnp.mean(x * x, axis=-1, keepdims=True) + EPS)


def _norm_kernel(h_ref, g_ref, o_ref):
    o_ref[...] = (_rms(h_ref[...]) * g_ref[...]).astype(bf16)


def _norm_gate_kernel(h_ref, g_ref, wg_ref, bg_ref, o_ref, gate_ref):
    xn = (_rms(h_ref[...]) * g_ref[...]).astype(bf16)
    o_ref[...] = xn
    gate_ref[...] = jax.nn.sigmoid(_dot(xn, wg_ref[...]) + bg_ref[...])


def _norm(h, g, tm):
    S, D = h.shape
    return pl.pallas_call(
        _norm_kernel,
        grid=(S // tm,),
        in_specs=[pl.BlockSpec((tm, D), lambda i: (i, 0)), pl.BlockSpec((1, D), lambda i: (0, 0))],
        out_specs=pl.BlockSpec((tm, D), lambda i: (i, 0)),
        out_shape=jax.ShapeDtypeStruct((S, D), bf16),
        compiler_params=_cparams("parallel"),
        name="norm",
    )(h, g)


def _norm_gate(h, g, wg, bg, tm):
    S, D = h.shape
    return pl.pallas_call(
        _norm_gate_kernel,
        grid=(S // tm,),
        in_specs=[
            pl.BlockSpec((tm, D), lambda i: (i, 0)),
            pl.BlockSpec((1, D), lambda i: (0, 0)),
            pl.BlockSpec((D, LANES), lambda i: (0, 0)),
            pl.BlockSpec((1, LANES), lambda i: (0, 0)),
        ],
        out_specs=[pl.BlockSpec((tm, D), lambda i: (i, 0)), pl.BlockSpec((tm, LANES), lambda i: (i, 0))],
        out_shape=[jax.ShapeDtypeStruct((S, D), bf16), jax.ShapeDtypeStruct((S, LANES), f32)],
        compiler_params=_cparams("parallel"),
        name="norm_gate",
    )(h, g, wg, bg)


PROJ_TN = 512
PROJ_HEADS = PROJ_TN // HEAD_DIM


def _proj_kernel(x_ref, w_ref, cosf_ref, sina_ref, sinb_ref, o_ref):
    j = pl.program_id(1)
    acc = _dot(x_ref[...], w_ref[...])
    tables = (cosf_ref[...], sina_ref[...], sinb_ref[...])
    q_tiles = N_HEADS_NSA // PROJ_HEADS
    k_width = N_KV_NSA * HEAD_DIM
    is_q = j < q_tiles
    is_kv = (j == COL_KS // PROJ_HEADS) | (j == COL_KW // PROJ_HEADS)
    is_sbq = (j >= COL_SBQ // PROJ_HEADS) & (j < COL_SBK // PROJ_HEADS)

    @pl.when(is_q)
    def _():
        o_ref[...] = (_rope(acc, *tables) * SCALE).astype(bf16)

    @pl.when(is_kv)
    def _():
        o_ref[:, :k_width] = _rope(acc[:, :k_width], *tables).astype(bf16)
        o_ref[:, k_width:] = acc[:, k_width:].astype(bf16)

    @pl.when(is_sbq)
    def _():
        o_ref[...] = (acc * SCALE).astype(bf16)

    @pl.when(jnp.logical_not(is_q | is_kv | is_sbq))
    def _():
        o_ref[...] = acc.astype(bf16)


def _proj(xn, w, cosf, sina, sinb, tm):
    S, D = xn.shape
    N = w.shape[1]
    tn = PROJ_TN
    assert COL_VS == COL_KS + N_KV_NSA and COL_KS % PROJ_HEADS == 0 and COL_KW % PROJ_HEADS == 0
    table = pl.BlockSpec((tm, HEAD_DIM), lambda i, j: (i, 0))
    return pl.pallas_call(
        _proj_kernel,
        grid=(S // tm, N // tn),
        in_specs=[
            pl.BlockSpec((tm, D), lambda i, j: (i, 0)),
            pl.BlockSpec((D, tn), lambda i, j: (0, j)),
            table, table, table,
        ],
        out_specs=pl.BlockSpec((tm, tn), lambda i, j: (i, j)),
        out_shape=jax.ShapeDtypeStruct((S, N), bf16),
        compiler_params=_cparams("parallel", "parallel"),
        name="proj",
    )(xn, w, cosf, sina, sinb)


def _gelu_tanh(x):
    return 0.5 * x * (1.0 + jnp.tanh(np.sqrt(2.0 / np.pi) * (x + 0.044715 * (x * x * x))))


def _compress_kernel(t_ref, w1_ref, w2_ref, cp_ref, cosf_ref, sina_ref, sinb_ref, o_ref):
    i = pl.program_id(0)
    t = t_ref[0]
    half = CMP_STRIDE * HEAD_DIM
    a = _dot(t, w1_ref[0, :half, :])
    b = _dot(t, w1_ref[0, half:, :])
    n_pad = a.shape[0]
    bias = _dot(cp_ref[...], w1_ref[0])[0:1, :]
    hid = a + pltpu.roll(b, n_pad - 1, axis=0) + bias
    c = _dot(_gelu_tanh(hid).astype(bf16), w2_ref[0])

    @pl.when(i < N_KV_NSA)
    def _():
        o_ref[0] = _rope(c, cosf_ref[...], sina_ref[...], sinb_ref[...]).astype(bf16)

    @pl.when(i >= N_KV_NSA)
    def _():
        o_ref[0] = c.astype(bf16)


def _compress(t2, w1, w2, cp, ccosf, csina, csinb):
    n4, n_pad, width = t2.shape
    return pl.pallas_call(
        _compress_kernel,
        grid=(n4,),
        in_specs=[
            pl.BlockSpec((1, n_pad, width), lambda i: (i, 0, 0)),
            pl.BlockSpec((1, 2 * width, CMP_HIDDEN), lambda i: (i // N_KV_NSA, 0, 0)),
            pl.BlockSpec((1, CMP_HIDDEN, HEAD_DIM), lambda i: (i // N_KV_NSA, 0, 0)),
            pl.BlockSpec((SUBLANES, 2 * width), lambda i: (0, 0)),
            pl.BlockSpec((n_pad, HEAD_DIM), lambda i: (0, 0)),
            pl.BlockSpec((n_pad, HEAD_DIM), lambda i: (0, 0)),
            pl.BlockSpec((n_pad, HEAD_DIM), lambda i: (0, 0)),
        ],
        out_specs=pl.BlockSpec((1, n_pad, HEAD_DIM), lambda i: (i, 0, 0)),
        out_shape=jax.ShapeDtypeStruct((n4, n_pad, HEAD_DIM), bf16),
        compiler_params=_cparams("parallel"),
        name="compress",
    )(t2, w1, w2, cp, ccosf, csina, csinb)


def _cmp_topk_kernel(q_ref, kc_ref, vc_ref, m_ref, o_ref, selb_ref, key_ref, *, n_slc):
    i = pl.program_id(1)
    t0 = i * Q_BLK
    rows = GROUP * Q_BLK
    kc = kc_ref[0]
    vc = vc_ref[0]
    n_pad = kc.shape[0]
    q4 = jnp.concatenate([q_ref[:, r * HEAD_DIM:(r + 1) * HEAD_DIM] for r in range(GROUP)], axis=0)
    s = _dot_nt(q4, kc)
    t = t0 + (lax.broadcasted_iota(jnp.int32, (rows, n_pad), 0) & (Q_BLK - 1))
    cmp_end = lax.broadcasted_iota(jnp.int32, (rows, n_pad), 1) * CMP_STRIDE + (CMP_BLOCK - 1)
    mask = cmp_end <= t
    s = jnp.where(mask, s, NEG_INF)
    m = jnp.max(s, axis=-1, keepdims=True)
    p = jnp.where(mask, jnp.exp(s - m), 0.0)
    l = jnp.sum(p, axis=-1, keepdims=True)
    p = p / jnp.where(l > 0.0, l, 1.0)
    o = _dot(p.astype(bf16), vc)
    for r in range(GROUP):
        o_ref[:, r * HEAD_DIM:(r + 1) * HEAD_DIM] = o[r * Q_BLK:(r + 1) * Q_BLK, :]

    p_imp = p[0:Q_BLK]
    for r in range(1, GROUP):
        p_imp = p_imp + p[r * Q_BLK:(r + 1) * Q_BLK]
    p_slc_t = _split_dot(p_imp, m_ref[...]).T

    blk = lax.broadcasted_iota(jnp.int32, (LANES, Q_BLK), 0)
    blk_t = (t0 + lax.broadcasted_iota(jnp.int32, (LANES, Q_BLK), 1)) >> SLC_SHIFT
    causal = blk <= blk_t
    forced = (blk == 0) | (blk == blk_t) | (blk == blk_t - 1)
    imp = jnp.where(causal, jnp.where(forced, FORCE_SCORE, p_slc_t), NEG_SCORE)
    key_ref[...] = imp

    def body(b2, rank):
        for u in range(2):
            b = b2 * 2 + u
            row = key_ref[pl.ds(b, 1), :]
            ahead = (row > imp) | ((row == imp) & (b < blk))
            rank = rank + jnp.where(ahead, 1.0, 0.0)
        return rank

    n_live = jnp.minimum((t0 + Q_BLK - 1) // SLC_BLOCK + 1, n_slc)
    rank = lax.fori_loop(0, (n_live + 1) // 2, body, jnp.zeros((LANES, Q_BLK), f32))
    sel = (rank < float(min(SLC_TOP_N, n_slc))) & causal
    selb_ref[0] = jnp.where(sel, 0.0, SEL_MASK_BIAS).T.astype(bf16)


def _cmp_topk(P, kvc, m_mat, n_slc):
    S = P.shape[0]
    n_pad = kvc.shape[1]
    gw = GROUP * HEAD_DIM
    return pl.pallas_call(
        functools.partial(_cmp_topk_kernel, n_slc=n_slc),
        grid=(N_KV_NSA, S // Q_BLK),
        in_specs=[
            pl.BlockSpec((Q_BLK, gw), lambda g, i: (i, g)),
            pl.BlockSpec((1, n_pad, HEAD_DIM), lambda g, i: (g, 0, 0)),
            pl.BlockSpec((1, n_pad, HEAD_DIM), lambda g, i: (N_KV_NSA + g, 0, 0)),
            pl.BlockSpec((n_pad, LANES), lambda g, i: (0, 0)),
        ],
        out_specs=[
            pl.BlockSpec((Q_BLK, gw), lambda g, i: (i, g)),
            pl.BlockSpec((1, Q_BLK, LANES), lambda g, i: (g, i, 0)),
        ],
        out_shape=[
            jax.ShapeDtypeStruct((S, N_HEADS_NSA * HEAD_DIM), f32),
            jax.ShapeDtypeStruct((N_KV_NSA, S, LANES), bf16),
        ],
        scratch_shapes=[pltpu.VMEM((LANES, Q_BLK), f32)],
        compiler_params=_cparams("parallel", "parallel"),
        name="cmp_topk",
    )(P, kvc, kvc, m_mat)


def _nsa_kernel(q_ref, ks_ref, vs_ref, kw_ref, vw_ref, e_ref, selb_ref, ocmp_ref, gate_ref,
                gg_ref, o_ref, *, q_blk, kv_blk, win_span):
    i = pl.program_id(1)
    t0 = i * q_blk
    rows = GROUP * q_blk
    S = ks_ref.shape[0]
    q4 = jnp.concatenate([q_ref[:, r * HEAD_DIM:(r + 1) * HEAD_DIM] for r in range(GROUP)], axis=0)
    selb = selb_ref[0]
    q_aug = jnp.concatenate([q4, jnp.concatenate([selb] * GROUP, axis=0)], axis=1)
    half = kv_blk // 2
    ones = jnp.ones((half, HEAD_DIM), bf16)

    def tile(kb, carry, causal_mask):
        m, acc = carry
        s = []
        for hf in range(2):
            k0 = pl.multiple_of(kb * kv_blk + hf * half, half)
            k_aug = jnp.concatenate([ks_ref[pl.ds(k0, half), :], e_ref[pl.ds(k0, half), :]], axis=1)
            s_h = _dot_nt(q_aug, k_aug)
            if causal_mask:
                t = t0 + (lax.broadcasted_iota(jnp.int32, (rows, half), 0) & (q_blk - 1))
                s_h = jnp.where(k0 + lax.broadcasted_iota(jnp.int32, (rows, half), 1) <= t, s_h, NEG_INF)
            s.append(s_h)
        for hf in range(2):
            k0 = pl.multiple_of(kb * kv_blk + hf * half, half)
            m_new = jnp.maximum(m, jnp.max(s[hf], axis=-1, keepdims=True))
            p = jnp.exp(s[hf] - m_new).astype(bf16)
            v_aug = jnp.concatenate([vs_ref[pl.ds(k0, half), :], ones], axis=1)
            acc = jnp.exp(m - m_new) * acc + _dot(p, v_aug)
            m = m_new
        return m, acc

    kd = t0 // kv_blk
    init = (jnp.full((rows, 1), NEG_INF, f32), jnp.zeros((rows, 2 * HEAD_DIM), f32))
    carry = lax.fori_loop(0, kd, lambda kb, c: tile(kb, c, False), init)
    _, acc = tile(kd, carry, True)
    o_slc = acc[:, :HEAD_DIM] / acc[:, HEAD_DIM:]

    w0 = pl.multiple_of(jnp.clip(t0 - WINDOW, 0, S - win_span), q_blk)
    s = _dot_nt(q4, kw_ref[pl.ds(w0, win_span), :])
    t = t0 + (lax.broadcasted_iota(jnp.int32, (rows, win_span), 0) & (q_blk - 1))
    diff = t - (w0 + lax.broadcasted_iota(jnp.int32, (rows, win_span), 1))
    s = jnp.where((diff >= 0) & (diff < WINDOW), s, NEG_INF)
    p = jnp.exp(s - jnp.max(s, axis=-1, keepdims=True))
    l_w = jnp.sum(p, axis=-1, keepdims=True)
    o_win = _dot(p.astype(bf16), vw_ref[pl.ds(w0, win_span), :]) / l_w

    gates = gate_ref[...]
    g = pl.program_id(0)
    for r in range(GROUP):
        rs = slice(r * q_blk, (r + 1) * q_blk)
        cs = slice(r * HEAD_DIM, (r + 1) * HEAD_DIM)
        h_lane = lax.broadcasted_iota(jnp.int32, gates.shape, 1) - (g * GROUP + r)

        def gate_col(branch):
            return jnp.sum(jnp.where(h_lane == branch * N_HEADS_NSA, gates, 0.0), axis=-1, keepdims=True)

        o = gate_col(0) * ocmp_ref[:, cs] + gate_col(1) * o_slc[rs] + gate_col(2) * o_win[rs]
        o_ref[:, cs] = (_rms(o) * gg_ref[:, cs]).astype(bf16)


def _nsa(P, e_mat, selb, o_cmp, gates, gg, q_blk, kv_blk, win_span):
    S = P.shape[0]
    gw = GROUP * HEAD_DIM
    full = lambda c: pl.BlockSpec((S, HEAD_DIM), lambda g, i, c=c: (0, c + g))
    return pl.pallas_call(
        functools.partial(_nsa_kernel, q_blk=q_blk, kv_blk=kv_blk, win_span=win_span),
        grid=(N_KV_NSA, S // q_blk),
        in_specs=[
            pl.BlockSpec((q_blk, gw), lambda g, i: (i, g)),
            full(COL_KS), full(COL_VS), full(COL_KW), full(COL_VW),
            pl.BlockSpec((S, LANES), lambda g, i: (0, 0)),
            pl.BlockSpec((1, q_blk, LANES), lambda g, i: (g, i, 0)),
            pl.BlockSpec((q_blk, gw), lambda g, i: (i, g)),
            pl.BlockSpec((q_blk, LANES), lambda g, i: (i, 0)),
            pl.BlockSpec((1, gw), lambda g, i: (0, g)),
        ],
        out_specs=pl.BlockSpec((q_blk, gw), lambda g, i: (i, g)),
        out_shape=jax.ShapeDtypeStruct((S, N_HEADS_NSA * HEAD_DIM), bf16),
        compiler_params=_cparams("parallel", "parallel"),
        name="nsa",
    )(P, P, P, P, P, e_mat, selb, o_cmp, gates, gg)


def _sb_span(q, k, v, u, t0, k0, carry, limit=None):
    n = k.shape[0] // SB_KV_BLK
    z = _dot_nt(q, k)
    t = t0 + lax.broadcasted_iota(jnp.int32, z.shape, 0)
    if limit is not None:
        t = jnp.minimum(t, limit)
    valid = (k0 + lax.broadcasted_iota(jnp.int32, z.shape, 1)) < t
    log_1m = jnp.where(valid, -(jnp.maximum(z, 0.0) + jnp.log1p(jnp.exp(-jnp.abs(z)))), 0.0)
    after = [None] * n
    for c in reversed(range(n)):
        part = log_1m[:, c * SB_KV_BLK:(c + 1) * SB_KV_BLK]
        after[c] = _split_dot(part, u) + carry
        carry = carry + jnp.sum(part, axis=-1, keepdims=True)
    after = jnp.concatenate(after, axis=1) if n > 1 else after[0]
    a = jnp.where(valid, jnp.exp(z + log_1m + after), 0.0)
    return _dot(a.astype(bf16), v), carry


def _sb_kernel(q_ref, k_ref, v_ref, u_ref, gg_ref, o_ref):
    i = pl.program_id(1)
    t0 = i * Q_BLK
    u = u_ref[...]
    span = SB_SPAN * SB_KV_BLK
    heads = [slice(h * HEAD_DIM, (h + 1) * HEAD_DIM) for h in range(SB_HEADS)]
    qs = [q_ref[:, hs] for hs in heads]

    s0 = pl.multiple_of(jnp.maximum(t0 - (SB_SPAN - 1) * SB_KV_BLK, 0), SB_KV_BLK)
    zero = jnp.zeros((Q_BLK, 1), f32)
    accs, carries = [], []
    for h, hs in enumerate(heads):
        acc, carry = _sb_span(qs[h], k_ref[pl.ds(s0, span), hs], v_ref[pl.ds(s0, span), hs], u, t0, s0, zero)
        accs.append(acc)
        carries.append(carry)

    def carry_max(cs):
        return jnp.max(functools.reduce(jnp.maximum, cs))

    def cond(c):
        nt, cmax, _, _ = c
        return (nt >= 0) & (cmax > SB_UNDERFLOW)

    def body(c):
        nt, _, cs, acs = c
        k0 = pl.multiple_of(jnp.maximum(nt - (SB_LOOP_SPAN - 1), 0) * SB_KV_BLK, SB_KV_BLK)
        limit = (nt + 1) * SB_KV_BLK
        width = SB_LOOP_SPAN * SB_KV_BLK
        new_cs, new_acs = [], []
        for h, hs in enumerate(heads):
            acc, carry = _sb_span(qs[h], k_ref[pl.ds(k0, width), hs], v_ref[pl.ds(k0, width), hs],
                                  u, t0, k0, cs[h], limit)
            new_cs.append(carry)
            new_acs.append(acs[h] + acc)
        return nt - SB_LOOP_SPAN, carry_max(new_cs), tuple(new_cs), tuple(new_acs)

    init = (s0 // SB_KV_BLK - 1, carry_max(carries), tuple(carries), tuple(accs))
    _, _, _, accs = lax.while_loop(cond, body, init)
    for h, hs in enumerate(heads):
        o_ref[:, hs] = (_rms(accs[h]) * gg_ref[:, hs]).astype(bf16)


def _sb(P, u_mat, gg):
    S = P.shape[0]
    hw = SB_HEADS * HEAD_DIM
    return pl.pallas_call(
        _sb_kernel,
        grid=(N_HEADS_SB // SB_HEADS, S // Q_BLK),
        in_specs=[
            pl.BlockSpec((Q_BLK, hw), lambda h, i: (i, COL_SBQ // SB_HEADS + h)),
            pl.BlockSpec((S, hw), lambda h, i: (0, COL_SBK // SB_HEADS + h)),
            pl.BlockSpec((S, hw), lambda h, i: (0, COL_SBV // SB_HEADS + h)),
            pl.BlockSpec((SB_KV_BLK, SB_KV_BLK), lambda h, i: (0, 0)),
            pl.BlockSpec((1, hw), lambda h, i: (0, N_HEADS_NSA // SB_HEADS + h)),
        ],
        out_specs=pl.BlockSpec((Q_BLK, hw), lambda h, i: (i, h)),
        out_shape=jax.ShapeDtypeStruct((S, N_HEADS_SB * HEAD_DIM), bf16),
        compiler_params=_cparams("parallel", "parallel"),
        name="sb",
    )(P, P, P, u_mat, gg)


def _out_proj_kernel(h_ref, a_ref, b_ref, wa_ref, wb_ref, o_ref):
    o_ref[...] = h_ref[...] + _dot(a_ref[...], wa_ref[...]) + _dot(b_ref[...], wb_ref[...])


def _out_proj(h, o_nsa, o_sb, w_out, tm, tn):
    S, D = h.shape
    ka = o_nsa.shape[1]
    kb = o_sb.shape[1]
    return pl.pallas_call(
        _out_proj_kernel,
        grid=(S // tm, D // tn),
        in_specs=[
            pl.BlockSpec((tm, tn), lambda i, j: (i, j)),
            pl.BlockSpec((tm, ka), lambda i, j: (i, 0)),
            pl.BlockSpec((tm, kb), lambda i, j: (i, 0)),
            pl.BlockSpec((ka, tn), lambda i, j: (0, j)),
            pl.BlockSpec((kb, tn), lambda i, j: (ka // kb, j)),
        ],
        out_specs=pl.BlockSpec((tm, tn), lambda i, j: (i, j)),
        out_shape=jax.ShapeDtypeStruct((S, D), f32),
        compiler_params=_cparams("parallel", "parallel"),
        name="out_proj",
    )(h, o_nsa, o_sb, w_out, w_out)


def _ffn_kernel(xn_ref, h_ref, wg_ref, wu_ref, wd_ref, gf_ref, o_ref, *, final_norm):
    f = pl.program_id(1)

    @pl.when(f == 0)
    def _():
        o_ref[...] = h_ref[...]

    xn = xn_ref[...]
    gate = _dot(xn, wg_ref[...])
    up = _dot(xn, wu_ref[...])
    act = (gate * jax.nn.sigmoid(gate) * up).astype(bf16)
    o_ref[...] += _dot(act, wd_ref[...])

    if final_norm:
        @pl.when(f == pl.num_programs(1) - 1)
        def _():
            o_ref[...] = _rms(o_ref[...]) * gf_ref[...]


def _ffn(xn, h, wg, wu, wd, gf, tm, tf, final_norm):
    S, D = h.shape
    F = wg.shape[1]
    return pl.pallas_call(
        functools.partial(_ffn_kernel, final_norm=final_norm),
        grid=(S // tm, F // tf),
        in_specs=[
            pl.BlockSpec((tm, D), lambda i, f: (i, 0)),
            pl.BlockSpec((tm, D), lambda i, f: (i, 0)),
            pl.BlockSpec((D, tf), lambda i, f: (0, f)),
            pl.BlockSpec((D, tf), lambda i, f: (0, f)),
            pl.BlockSpec((tf, D), lambda i, f: (f, 0)),
            pl.BlockSpec((1, D), lambda i, f: (0, 0)),
        ],
        out_specs=pl.BlockSpec((tm, D), lambda i, f: (i, 0)),
        out_shape=jax.ShapeDtypeStruct((S, D), f32),
        compiler_params=_cparams("parallel", "arbitrary"),
        name="ffn",
    )(xn, h, wg, wu, wd, gf)


def _rope_tables(pos):
    inv = ROPE_THETA ** (-jnp.arange(0, ROPE_DIM, 2, dtype=f32) / ROPE_DIM)
    ang = pos.astype(f32)[:, None] * inv
    cos, sin = jnp.cos(ang), jnp.sin(ang)
    n = pos.shape[0]
    rest = HEAD_DIM - ROPE_DIM
    cosf = jnp.concatenate([cos, cos, jnp.ones((n, rest), f32)], axis=1)
    sina = jnp.concatenate([-sin, jnp.zeros((n, HEAD_DIM - ROPE_HALF), f32)], axis=1)
    sinb = jnp.concatenate([jnp.zeros((n, ROPE_HALF), f32), sin, jnp.zeros((n, rest), f32)], axis=1)
    return cosf, sina, sinb


def _importance_matrix(n_pad):
    ratio = SLC_BLOCK // CMP_STRIDE
    nb = CMP_BLOCK // CMP_STRIDE
    m = np.zeros((n_pad, LANES), np.float32)
    for n in range(n_pad):
        for k in range(nb):
            j = (n + k) // ratio
            if j < LANES:
                m[n, j] += 1.0
    return m


def kernel(x, positions, norm_mix, w_in, b_gate, cmp_pos, w_cmp_k1, w_cmp_k2, w_cmp_v1, w_cmp_v2,
           norm_grp, w_out, norm_ffn, w_gate, w_up, w_down, norm_final):
    B, S, D = x.shape
    assert B == 1 and S % KV_BLK == 0 and S // SLC_BLOCK <= LANES
    depth = w_in.shape[0]
    n_slc = S // SLC_BLOCK
    n_pad = S // CMP_STRIDE
    win_span = WINDOW + NSA_Q_BLK
    tm = 512
    tm_proj = 1024 if S % 1024 == 0 else tm
    tf = 512
    pos = positions[0]

    cosf, sina, sinb = _rope_tables(pos)
    cmp_end = jnp.minimum(jnp.arange(n_pad) * CMP_STRIDE + CMP_BLOCK - 1, S - 1)
    ccosf, csina, csinb = _rope_tables(pos[cmp_end])

    m_mat = jnp.asarray(_importance_matrix(n_pad), bf16)
    e_np = (np.arange(S)[:, None] // SLC_BLOCK == np.arange(LANES)[None, :])
    e_mat = jnp.asarray(e_np, bf16)
    u_np = np.arange(SB_KV_BLK)[:, None] > np.arange(SB_KV_BLK)[None, :]
    u_mat = jnp.asarray(u_np, bf16)

    q_end = N_HEADS_NSA * HEAD_DIM + 6 * N_KV_NSA * HEAD_DIM
    n_gate = N_BRANCH * N_HEADS_NSA
    w_main = jnp.concatenate([w_in[:, :, :q_end], w_in[:, :, q_end + n_gate:]], axis=2).astype(bf16)
    w_g = jnp.pad(w_in[:, :, q_end:q_end + n_gate], ((0, 0), (0, 0), (0, LANES - n_gate))).astype(bf16)
    b_g = jnp.pad(b_gate, ((0, 0), (0, LANES - n_gate)))[:, None, :]
    w1 = jnp.stack([w_cmp_k1, w_cmp_v1], axis=1).astype(bf16)
    w2 = jnp.stack([w_cmp_k2, w_cmp_v2], axis=1).astype(bf16)
    cp = jnp.broadcast_to(cmp_pos.reshape(depth, 1, CMP_BLOCK * HEAD_DIM),
                          (depth, SUBLANES, CMP_BLOCK * HEAD_DIM)).astype(bf16)
    w_out_b = w_out.astype(bf16)
    w_gate_b, w_up_b, w_down_b = w_gate.astype(bf16), w_up.astype(bf16), w_down.astype(bf16)

    h = x[0]
    for layer in range(depth):
        xn, gates = _norm_gate(h, norm_mix[layer][None, :], w_g[layer], b_g[layer], tm)
        P = _proj(xn, w_main[layer], cosf, sina, sinb, tm_proj)
        n_c = 2 * N_KV_NSA
        t2 = P[:, COL_KC * HEAD_DIM:(COL_KC + n_c) * HEAD_DIM].reshape(S, n_c, HEAD_DIM)
        t2 = t2.transpose(1, 0, 2).reshape(n_c, n_pad, CMP_STRIDE * HEAD_DIM)
        kvc = _compress(t2, w1[layer], w2[layer], cp[layer], ccosf, csina, csinb)
        o_cmp, selb = _cmp_topk(P, kvc, m_mat, n_slc)
        gg = norm_grp[layer][None, :]
        o_nsa = _nsa(P, e_mat, selb, o_cmp, gates, gg, NSA_Q_BLK, KV_BLK, win_span)
        o_sb = _sb(P, u_mat, gg)
        h = _out_proj(h, o_nsa, o_sb, w_out_b[layer], tm_proj, 512)
        xn = _norm(h, norm_ffn[layer][None, :], tm)
        h = _ffn(xn, h, w_gate_b[layer], w_up_b[layer], w_down_b[layer],
                 norm_final[None, :], tm, tf, layer == depth - 1)
    return h[None]
```
